```python
import functools
import jax, jax.numpy as jnp
from jax import lax
import numpy as np

D_MODEL = 2048
BATCH = 8
SEQ = 2048
DEPTH = 1
DEC_BATCH = 32
DEC_SEQ = 4
PAST_LEN = 16384
PAGE_SIZE = 128

EPS = 1e-6
D_CONV = 1024
N_HEADS = 8
HEAD_DIM = 128
D_ATTN = N_HEADS * HEAD_DIM
CONV_K = 3
D_FF = 5632
Q_BLOCK = 128
IN_SIZES = (D_CONV, D_CONV, D_CONV, D_ATTN, D_ATTN, D_ATTN, N_HEADS, D_MODEL, D_MODEL)
D_IN = sum(IN_SIZES)

kernel_name = "hybrid_shortconv_fox_convffn_step"


def rmsnorm(x, g):
    xf = x.astype(jnp.float32)
    var = jnp.mean(xf * xf, axis=-1, keepdims=True)
    return (xf * lax.rsqrt(var + EPS) * g.astype(jnp.float32)).astype(x.dtype)


def causal_dwconv(u, buf, w):
    t = u.shape[1]
    ext = jnp.concatenate([buf.astype(u.dtype), u], axis=1)
    y = ext[:, 0:t] * w[0]
    for j in range(1, CONV_K):
        y = y + ext[:, j:j + t] * w[j]
    return y, ext[:, -(CONV_K - 1):]


def split_in(z):
    idx = np.cumsum(IN_SIZES)[:-1].tolist()
    return jnp.split(z, idx, axis=-1)


def fox_prompt(q, k, v, lf):
    b, t, h, d = q.shape
    scale = d ** -0.5
    f_cum = jnp.cumsum(lf, axis=1).transpose(0, 2, 1)
    nblk = t // Q_BLOCK
    qb = q.reshape(b, nblk, Q_BLOCK, h, d).transpose(1, 0, 2, 3, 4)
    fb = f_cum.reshape(b, h, nblk, Q_BLOCK).transpose(2, 0, 1, 3)
    pos_k = jnp.arange(t)

    def one_block(args):
        i, qi, fi = args
        s = jnp.einsum('bqhd,bkhd->bhqk', qi, k, preferred_element_type=jnp.float32) * scale
        s = s + (fi[..., :, None] - f_cum[..., None, :])
        pos_q = i * Q_BLOCK + jnp.arange(Q_BLOCK)
        s = jnp.where(pos_k[None, :] <= pos_q[:, None], s, -jnp.inf)
        p = jax.nn.softmax(s, axis=-1)
        return jnp.einsum('bhqk,bkhd->bqhd', p.astype(v.dtype), v)

    out = lax.map(one_block, (jnp.arange(nblk), qb, fb))
    return out.transpose(1, 0, 2, 3, 4).reshape(b, t, h, d)


def fox_sample(q, k, v, lf, cache_k, cache_v, cache_lf, page_table):
    t, h, d = q.shape[1], q.shape[2], q.shape[3]
    scale = d ** -0.5
    causal = jnp.tril(jnp.ones((t, t), dtype=bool))

    def one_seq(args):
        qi, ki, vi, lfi, pages = args
        kp = cache_k[pages].reshape(-1, h, d)
        vp = cache_v[pages].reshape(-1, h, d)
        lfp = cache_lf[pages].reshape(-1, h).astype(jnp.float32)
        rc = lax.cumsum(lfp, axis=0, reverse=True)
        suffix = jnp.concatenate([rc[1:], jnp.zeros((1, h), jnp.float32)], axis=0)
        fn = jnp.cumsum(lfi, axis=0).T
        s_past = jnp.einsum('qhd,khd->hqk', qi, kp, preferred_element_type=jnp.float32) * scale
        s_past = s_past + fn[:, :, None] + suffix.T[:, None, :]
        s_new = jnp.einsum('qhd,khd->hqk', qi, ki, preferred_element_type=jnp.float32) * scale
        s_new = jnp.where(causal, s_new + fn[:, :, None] - fn[:, None, :], -jnp.inf)
        p = jax.nn.softmax(jnp.concatenate([s_past, s_new], axis=-1), axis=-1)
        n_past = kp.shape[0]
        o = jnp.einsum('hqk,khd->qhd', p[..., :n_past].astype(vp.dtype), vp)
        return o + jnp.einsum('hqk,khd->qhd', p[..., n_past:].astype(vi.dtype), vi)

    return lax.map(one_seq, (q, k, v, lf, page_table))


def layer(x, conv_buf, ffn_buf, attend, norm_mix_g, w_in, b_f, conv_a_w, w_out_a, w_out_b, w_o,
          norm_ffn_g, w_up, w_gate, conv_ffn_w, w_down):
    b, t, _ = x.shape
    xn = rmsnorm(x, norm_mix_g)
    h_a, b_a, c_a, q, k, v, f_logit, g_a, g_b = split_in(xn @ w_in)
    ca, new_conv = causal_dwconv(c_a * h_a, conv_buf, conv_a_w)
    a = (b_a * ca) @ w_out_a
    q = q.reshape(b, t, N_HEADS, HEAD_DIM)
    k = k.reshape(b, t, N_HEADS, HEAD_DIM)
    v = v.reshape(b, t, N_HEADS, HEAD_DIM)
    lf = jax.nn.log_sigmoid(f_logit.astype(jnp.float32) + b_f.astype(jnp.float32))
    o = attend(q, k, v, lf).reshape(b, t, D_ATTN) @ w_out_b
    m = jax.nn.sigmoid(g_a) * a + jax.nn.sigmoid(g_b) * o
    x = x + m @ w_o
    xn = rmsnorm(x, norm_ffn_g)
    u = xn @ w_up
    gc, new_ffn = causal_dwconv(xn @ w_gate, ffn_buf, conv_ffn_w)
    x = x + (jax.nn.silu(gc) * u) @ w_down
    return x, k, v, lf, new_conv, new_ffn


def setup_inputs(seed: int = 0) -> dict:
    key = jax.random.key(seed)
    ks = jax.random.split(key, 24)
    f32 = jnp.float32
    n_pages = PAST_LEN // PAGE_SIZE
    n_pool = (DEC_BATCH * n_pages * 5) // 4
    nrm = lambda k, s, sc: jax.random.normal(k, s, f32) * sc
    b_f = jnp.linspace(1.0, 6.0, N_HEADS, dtype=f32)[None, :] + nrm(ks[0], (DEPTH, N_HEADS), 0.1)
    cache_lf = jax.nn.log_sigmoid(b_f[:, None, None, :] + nrm(ks[1], (DEPTH, n_pool, PAGE_SIZE, N_HEADS), 1.0))
    page_table = jax.random.permutation(ks[2], n_pool)[:DEC_BATCH * n_pages].reshape(DEC_BATCH, n_pages).astype(jnp.int32)
    return {
        "x_prompt": nrm(ks[3], (BATCH, SEQ, D_MODEL), 1.0),
        "x_sample": nrm(ks[4], (DEC_BATCH, DEC_SEQ, D_MODEL), 1.0),
        "cache_k": nrm(ks[5], (DEPTH, n_pool, PAGE_SIZE, N_HEADS, HEAD_DIM), 1.0),
        "cache_v": nrm(ks[6], (DEPTH, n_pool, PAGE_SIZE, N_HEADS, HEAD_DIM), 1.0),
        "cache_lf": cache_lf,
        "state_conv_a": nrm(ks[7], (DEPTH, DEC_BATCH, CONV_K - 1, D_CONV), 1.0),
        "state_conv_ffn": nrm(ks[8], (DEPTH, DEC_BATCH, CONV_K - 1, D_FF), 1.0),
        "page_table": page_table,
        "norm_mix_g": 1.0 + nrm(ks[9], (DEPTH, D_MODEL), 0.02),
        "w_in": nrm(ks[10], (DEPTH, D_MODEL, D_IN), D_MODEL ** -0.5),
        "b_f": b_f,
        "conv_a_w": nrm(ks[11], (DEPTH, CONV_K, D_CONV), CONV_K ** -0.5),
        "w_out_a": nrm(ks[12], (DEPTH, D_CONV, D_MODEL), D_CONV ** -0.5),
        "w_out_b": nrm(ks[13], (DEPTH, D_ATTN, D_MODEL), D_ATTN ** -0.5),
        "w_o": nrm(ks[14], (DEPTH, D_MODEL, D_MODEL), D_MODEL ** -0.5),
        "norm_ffn_g": 1.0 + nrm(ks[15], (DEPTH, D_MODEL), 0.02),
        "w_up": nrm(ks[16], (DEPTH, D_MODEL, D_FF), D_MODEL ** -0.5),
        "w_gate": nrm(ks[17], (DEPTH, D_MODEL, D_FF), D_MODEL ** -0.5),
        "conv_ffn_w": nrm(ks[18], (DEPTH, CONV_K, D_FF), CONV_K ** -0.5),
        "w_down": nrm(ks[19], (DEPTH, D_FF, D_MODEL), D_FF ** -0.5),
        "norm_final_g": 1.0 + nrm(ks[20], (D_MODEL,), 0.02),
    }


def reference(x_prompt, x_sample, cache_k, cache_v, cache_lf, state_conv_a, state_conv_ffn, page_table,
              norm_mix_g, w_in, b_f, conv_a_w, w_out_a, w_out_b, w_o, norm_ffn_g, w_up, w_gate,
              conv_ffn_w, w_down, norm_final_g):
    hp, hs = x_prompt, x_sample
    bp = x_prompt.shape[0]
    kp_l, vp_l, lfp_l, ks_l, vs_l, lfs_l, cap_l, cas_l, cfp_l, cfs_l = ([] for _ in range(10))
    for l in range(DEPTH):
        params = (norm_mix_g[l], w_in[l], b_f[l], conv_a_w[l], w_out_a[l], w_out_b[l], w_o[l],
                  norm_ffn_g[l], w_up[l], w_gate[l], conv_ffn_w[l], w_down[l])
        zero_a = jnp.zeros((bp, CONV_K - 1, D_CONV), hp.dtype)
        zero_f = jnp.zeros((bp, CONV_K - 1, D_FF), hp.dtype)
        hp, kp, vp, lfp, cap, cfp = layer(hp, zero_a, zero_f, fox_prompt, *params)
        attend_s = functools.partial(fox_sample, cache_k=cache_k[l], cache_v=cache_v[l],
                                     cache_lf=cache_lf[l], page_table=page_table)
        hs, ks_, vs_, lfs, cas, cfs = layer(hs, state_conv_a[l], state_conv_ffn[l], attend_s, *params)
        kp_l.append(kp); vp_l.append(vp); lfp_l.append(lfp)
        ks_l.append(ks_); vs_l.append(vs_); lfs_l.append(lfs)
        cap_l.append(cap); cas_l.append(cas); cfp_l.append(cfp); cfs_l.append(cfs)
    y_prompt = rmsnorm(hp, norm_final_g)
    y_sample = rmsnorm(hs, norm_final_g)
    return (y_prompt, y_sample,
            jnp.stack(kp_l), jnp.stack(vp_l), jnp.stack(lfp_l),
            jnp.stack(ks_l), jnp.stack(vs_l), jnp.stack(lfs_l),
            jnp.stack(cap_l), jnp.stack(cas_l), jnp.stack(cfp_l), jnp.stack(cfs_l))
```

```python
import functools

import jax
import jax.numpy as jnp
from jax import lax
from jax.experimental import pallas as pl
from jax.experimental.pallas import tpu as pltpu

F32 = jnp.float32
BF16 = jnp.bfloat16
EPS = 1e-6
LANES = 128
SUBLANES = 8
VMEM_LIMIT = 56 * 1024 * 1024
NEG_BIG = -1e30


def _pick(n, cands):
    for c in cands:
        if n % c == 0:
            return c
    return n


def _params(n_axes):
    return pltpu.CompilerParams(dimension_semantics=("arbitrary",) * n_axes,
                                vmem_limit_bytes=VMEM_LIMIT)


def _dot(a, b):
    return jnp.dot(a, b, preferred_element_type=F32)


def _dot_nt(a, b):
    return lax.dot_general(a, b, (((1,), (1,)), ((), ())), preferred_element_type=F32)


def _split3(x):
    hi = x.astype(BF16)
    r1 = x - hi.astype(F32)
    mid = r1.astype(BF16)
    lo = (r1 - mid.astype(F32)).astype(BF16)
    return hi, mid, lo


def _dot01(ones_mat, x, ones_on_left):
    out = None
    for part in _split3(x):
        t = _dot(ones_mat, part) if ones_on_left else _dot(part, ones_mat)
        out = t if out is None else out + t
    return out


def _sigmoid(x):
    return 1.0 / (1.0 + jnp.exp(-x))


def _log_sigmoid(x):
    return jnp.minimum(x, 0.0) - jnp.log1p(jnp.exp(-jnp.abs(x)))


def _rms_scale(x, g):
    var = jnp.mean(x * x, axis=-1, keepdims=True)
    return x * lax.rsqrt(var + EPS) * g


def _rmsnorm_kernel(x_ref, g_ref, o_ref):
    o_ref[...] = _rms_scale(x_ref[...], g_ref[...]).astype(o_ref.dtype)


def _rmsnorm_bf16(x, g):
    m, d = x.shape
    tm = _pick(m, (512, 256, 128))
    return pl.pallas_call(
        _rmsnorm_kernel,
        grid=(m // tm,),
        in_specs=[pl.BlockSpec((tm, d), lambda i: (i, 0)),
                  pl.BlockSpec((1, d), lambda i: (0, 0))],
        out_specs=pl.BlockSpec((tm, d), lambda i: (i, 0)),
        out_shape=jax.ShapeDtypeStruct((m, d), BF16),
        compiler_params=_params(1),
        name="rmsnorm_bf16",
    )(x, g.reshape(1, d))


def _causal_conv(u, cw, tpos, prev):
    k = cw.shape[0]
    y = u * cw[k - 1:k, :]
    for j in range(1, k):
        shifted = pltpu.roll(u, j, axis=0)
        fill = 0.0 if prev is None else prev[j - 1]
        y = y + jnp.where(tpos >= j, shifted, fill) * cw[k - 1 - j:k - j, :]
    return y


def _mixa_prompt_kernel(xn_ref, wh_ref, wb_ref, wc_ref, cw_ref, ya_ref, tail_ref):
    xn = xn_ref[...]
    u = _dot(xn, wc_ref[...]) * _dot(xn, wh_ref[...])
    tpos = lax.broadcasted_iota(jnp.int32, u.shape, 0)
    y = _causal_conv(u, cw_ref[...], tpos, None)
    ya_ref[...] = (_dot(xn, wb_ref[...]) * y).astype(ya_ref.dtype)
    nt = tail_ref.shape[0]
    tail_ref[...] = u[u.shape[0] - nt:, :]


def _mixa_sample_kernel(xn_ref, wh_ref, wb_ref, wc_ref, cw_ref, p1_ref, p2_ref, ya_ref, u_ref, *, seq):
    xn = xn_ref[...]
    u = _dot(xn, wc_ref[...]) * _dot(xn, wh_ref[...])
    tpos = lax.broadcasted_iota(jnp.int32, u.shape, 0) % seq
    y = _causal_conv(u, cw_ref[...], tpos, [p1_ref[...], p2_ref[...]])
    ya_ref[...] = (_dot(xn, wb_ref[...]) * y).astype(ya_ref.dtype)
    u_ref[...] = u


def _prev_rows(state, seq):
    b, km1, c = state.shape
    outs = []
    for j in range(1, km1 + 1):
        rows = [state[:, km1 - j + t, :] if t < j else jnp.zeros((b, c), state.dtype) for t in range(seq)]
        outs.append(jnp.stack(rows, axis=1).reshape(b * seq, c))
    return outs


def _new_state(state, u, seq):
    b, km1, c = state.shape
    ext = jnp.concatenate([state, u.reshape(b, seq, c)], axis=1)
    return ext[:, -km1:, :]


def _mixa_prompt(xn, wh, wb, wc, cw, nb, seq):
    m, d = xn.shape
    c = wh.shape[1]
    km1 = cw.shape[0] - 1
    tn = _pick(c, (256, 128))
    wspec = pl.BlockSpec((d, tn), lambda b, j: (0, j))
    return pl.pallas_call(
        _mixa_prompt_kernel,
        grid=(nb, c // tn),
        in_specs=[pl.BlockSpec((seq, d), lambda b, j: (b, 0)), wspec, wspec, wspec,
                  pl.BlockSpec((km1 + 1, tn), lambda b, j: (0, j))],
        out_specs=[pl.BlockSpec((seq, tn), lambda b, j: (b, j)),
                   pl.BlockSpec((None, km1, tn), lambda b, j: (b, 0, j))],
        out_shape=[jax.ShapeDtypeStruct((m, c), BF16),
                   jax.ShapeDtypeStruct((nb, km1, c), F32)],
        compiler_params=_params(2),
        name="mixa_prompt",
    )(xn, wh, wb, wc, cw)


def _mixa_sample(xn, wh, wb, wc, cw, state, seq):
    m, d = xn.shape
    c = wh.shape[1]
    tn = _pick(c, (512, 256, 128))
    p1, p2 = _prev_rows(state, seq)
    wspec = pl.BlockSpec((d, tn), lambda j: (0, j))
    cspec = pl.BlockSpec((m, tn), lambda j: (0, j))
    ya, u = pl.pallas_call(
        functools.partial(_mixa_sample_kernel, seq=seq),
        grid=(c // tn,),
        in_specs=[pl.BlockSpec((m, d), lambda j: (0, 0)), wspec, wspec, wspec,
                  pl.BlockSpec((cw.shape[0], tn), lambda j: (0, j)), cspec, cspec],
        out_specs=[cspec, cspec],
        out_shape=[jax.ShapeDtypeStruct((m, c), BF16), jax.ShapeDtypeStruct((m, c), F32)],
        compiler_params=_params(1),
        name="mixa_sample",
    )(xn, wh, wb, wc, cw, p1, p2)
    return ya, _new_state(state, u, seq)


def _qkv_kernel(xn_ref, wq_ref, wk_ref, wv_ref, q_ref, k_ref, v_ref):
    xn = xn_ref[...]
    q_ref[...] = _dot(xn, wq_ref[...]).astype(q_ref.dtype)
    k_ref[...] = _dot(xn, wk_ref[...])
    v_ref[...] = _dot(xn, wv_ref[...])


def _qkv(xn, wq, wk, wv):
    m, d = xn.shape
    n = wq.shape[1]
    tm = _pick(m, (1024, 512, 256, 128))
    tn = _pick(n, (512, 256, 128))
    wspec = pl.BlockSpec((d, tn), lambda i, j: (0, j))
    ospec = pl.BlockSpec((tm, tn), lambda i, j: (i, j))
    return pl.pallas_call(
        _qkv_kernel,
        grid=(m // tm, n // tn),
        in_specs=[pl.BlockSpec((tm, d), lambda i, j: (i, 0)), wspec, wspec, wspec],
        out_specs=[ospec, ospec, ospec],
        out_shape=[jax.ShapeDtypeStruct((m, n), BF16), jax.ShapeDtypeStruct((m, n), F32),
                   jax.ShapeDtypeStruct((m, n), F32)],
        compiler_params=_params(2),
        name="qkv_proj",
    )(xn, wq, wk, wv)


def _gates_kernel(xn_ref, wa_ref, wb_ref, sa_ref, sb_ref):
    xn = xn_ref[...]
    sa_ref[...] = _sigmoid(_dot(xn, wa_ref[...])).astype(sa_ref.dtype)
    sb_ref[...] = _sigmoid(_dot(xn, wb_ref[...])).astype(sb_ref.dtype)


def _gates(xn, wa, wb):
    m, d = xn.shape
    n = wa.shape[1]
    tm = _pick(m, (1024, 512, 256, 128))
    tn = _pick(n, (512, 256, 128))
    wspec = pl.BlockSpec((d, tn), lambda i, j: (0, j))
    ospec = pl.BlockSpec((tm, tn), lambda i, j: (i, j))
    return pl.pallas_call(
        _gates_kernel,
        grid=(m // tm, n // tn),
        in_specs=[pl.BlockSpec((tm, d), lambda i, j: (i, 0)), wspec, wspec],
        out_specs=[ospec, ospec],
        out_shape=[jax.ShapeDtypeStruct((m, n), BF16), jax.ShapeDtypeStruct((m, n), BF16)],
        compiler_params=_params(2),
        name="gate_proj",
    )(xn, wa, wb)


def _logforget_kernel(xn_ref, wf_ref, bf_ref, lf_ref):
    lf_ref[...] = _log_sigmoid(_dot(xn_ref[...], wf_ref[...]) + bf_ref[...])


def _logforget(xn, wf_pad, bf_pad):
    m, d = xn.shape
    tm = _pick(m, (1024, 512, 256, 128))
    return pl.pallas_call(
        _logforget_kernel,
        grid=(m // tm,),
        in_specs=[pl.BlockSpec((tm, d), lambda i: (i, 0)),
                  pl.BlockSpec((d, LANES), lambda i: (0, 0)),
                  pl.BlockSpec((1, LANES), lambda i: (0, 0))],
        out_specs=pl.BlockSpec((tm, LANES), lambda i: (i, 0)),
        out_shape=jax.ShapeDtypeStruct((m, LANES), F32),
        compiler_params=_params(1),
        name="logforget",
    )(xn, wf_pad, bf_pad)


def _cumsum_kernel(lf_ref, f_ref, *, blk):
    t = lf_ref.shape[0]
    r = lax.broadcasted_iota(jnp.int32, (blk, blk), 0)
    c = lax.broadcasted_iota(jnp.int32, (blk, blk), 1)
    lower = (c <= r).astype(BF16)
    carry = jnp.zeros((1, lf_ref.shape[1]), F32)
    for i in range(t // blk):
        y = _dot01(lower, lf_ref[i * blk:(i + 1) * blk, :], True) + carry
        f_ref[i * blk:(i + 1) * blk, :] = y
        carry = y[blk - 1:blk, :]


def _cumsum_rows(lf, nb, seq):
    blk = _pick(seq, (256, 128))
    return pl.pallas_call(
        functools.partial(_cumsum_kernel, blk=blk),
        grid=(nb,),
        in_specs=[pl.BlockSpec((seq, LANES), lambda b: (b, 0))],
        out_specs=pl.BlockSpec((seq, LANES), lambda b: (b, 0)),
        out_shape=jax.ShapeDtypeStruct(lf.shape, F32),
        compiler_params=_params(1),
        name="logforget_cumsum",
    )(lf)


def _fox_prompt_kernel(q_ref, k_ref, v_ref, fcol_ref, frow_ref, o_ref, *, scale):
    h = pl.program_id(1)
    i = pl.program_id(2)
    tq, hd = q_ref.shape
    q = q_ref[...]
    lane = lax.broadcasted_iota(jnp.int32, fcol_ref.shape, 1)
    fq = jnp.sum(jnp.where(lane == h, fcol_ref[...], 0.0), axis=-1, keepdims=True)

    def block(kb, carry, diagonal):
        m, l, acc = carry
        off = pl.multiple_of(kb * tq, tq)
        kblk = k_ref[pl.ds(off, tq), :].astype(BF16)
        s = _dot_nt(q, kblk) * scale + (fq - frow_ref[:, pl.ds(off, tq)])
        if diagonal:
            rows = lax.broadcasted_iota(jnp.int32, s.shape, 0)
            cols = lax.broadcasted_iota(jnp.int32, s.shape, 1)
            s = jnp.where(cols <= rows, s, -jnp.inf)
        m_new = jnp.maximum(m, jnp.max(s, axis=-1, keepdims=True))
        alpha = jnp.exp(m - m_new)
        p = jnp.exp(s - m_new)
        l = alpha * l + jnp.sum(p, axis=-1, keepdims=True)
        acc = alpha * acc + _dot(p.astype(BF16), v_ref[pl.ds(off, tq), :].astype(BF16))
        return m_new, l, acc

    init = (jnp.full((tq, 1), NEG_BIG, F32), jnp.zeros((tq, 1), F32), jnp.zeros((tq, hd), F32))
    carry = block(i, init, True)
    _, l, acc = lax.fori_loop(0, i, lambda kb, c: block(kb, c, False), carry)
    o_ref[...] = (acc / l).astype(o_ref.dtype)


def _fox_prompt(q, k, v, fcum, nb, seq, nh, hd):
    m = q.shape[0]
    tq = _pick(seq, (512, 256, 128))
    nq = seq // tq
    frow = fcum[:, :nh].reshape(nb, seq, nh).transpose(0, 2, 1).reshape(nb * nh, 1, seq)
    return pl.pallas_call(
        functools.partial(_fox_prompt_kernel, scale=hd ** -0.5),
        grid=(nb, nh, nq),
        in_specs=[pl.BlockSpec((tq, hd), lambda b, h, i: (b * nq + i, h)),
                  pl.BlockSpec((seq, hd), lambda b, h, i: (b, h)),
                  pl.BlockSpec((seq, hd), lambda b, h, i: (b, h)),
                  pl.BlockSpec((tq, LANES), lambda b, h, i: (b * nq + i, 0)),
                  pl.BlockSpec((None, 1, seq), lambda b, h, i: (b * nh + h, 0, 0))],
        out_specs=pl.BlockSpec((tq, hd), lambda b, h, i: (b * nq + i, h)),
        out_shape=jax.ShapeDtypeStruct((m, nh * hd), BF16),
        compiler_params=_params(3),
        name="fox_prompt",
    )(q, k, v, fcum, frow)


def _suffix_kernel(lft_ref, o_ref):
    rows, page = lft_ref.shape
    r = lax.broadcasted_iota(jnp.int32, (page, 2 * page), 0)
    c = lax.broadcasted_iota(jnp.int32, (page, 2 * page), 1)
    sel = jnp.logical_or(c >= page, r > c).astype(BF16)
    y = _dot01(sel, lft_ref[...], False)
    nh = o_ref.shape[1] // 2
    o_ref[:, :nh, :] = y[:, :page].reshape(rows // nh, nh, page)
    o_ref[:, nh:, :] = y[:, page:].reshape(rows // nh, nh, page)


def _page_suffix(cache_lf_l):
    n_pool, page, nh = cache_lf_l.shape
    lft = cache_lf_l.transpose(0, 2, 1).reshape(n_pool * nh, page)
    pp = _pick(n_pool, (64, 32, 16, 8, 4, 2, 1))
    return pl.pallas_call(
        _suffix_kernel,
        grid=(n_pool // pp,),
        in_specs=[pl.BlockSpec((pp * nh, page), lambda i: (i, 0))],
        out_specs=pl.BlockSpec((pp, 2 * nh, page), lambda i: (i, 0, 0)),
        out_shape=jax.ShapeDtypeStruct((n_pool, 2 * nh, page), F32),
        compiler_params=_params(1),
        name="page_suffix",
    )(lft)


def _fox_sample_kernel(pt_ref, q_ref, kn_ref, vn_ref, lfn_ref, *rest, n_pg, nh, hd, scale):
    del pt_ref
    k_refs = rest[:n_pg]
    v_refs = rest[n_pg:2 * n_pg]
    w_refs = rest[2 * n_pg:3 * n_pg]
    o_ref = rest[3 * n_pg]
    qbd_ref, fn_ref, m_ref, l_ref, acc_ref, tail_ref, kcat_ref, vcat_ref = rest[3 * n_pg + 1:]
    g = pl.program_id(1)
    t_new, width = q_ref.shape
    page = k_refs[0].shape[0]

    @pl.when(g == 0)
    def _init():
        head_of_lane = lax.broadcasted_iota(jnp.int32, (nh, width), 1) // hd
        head_of_row = lax.broadcasted_iota(jnp.int32, (nh, width), 0)
        own = head_of_lane == head_of_row
        eye = (lax.broadcasted_iota(jnp.int32, (nh, LANES), 0)
               == lax.broadcasted_iota(jnp.int32, (nh, LANES), 1))
        qf = q_ref[...].astype(F32)
        kn = kn_ref[...]
        vn = vn_ref[...]
        lfn = lfn_ref[...]
        qb, fcol = [], []
        run = jnp.zeros((1, LANES), F32)
        for t in range(t_new):
            qb.append(jnp.where(own, jnp.broadcast_to(qf[t:t + 1, :], (nh, width)), 0.0))
            run = run + lfn[t:t + 1, :]
            fcol.append(jnp.sum(jnp.where(eye, jnp.broadcast_to(run, (nh, LANES)), 0.0),
                                axis=-1, keepdims=True))
        for t in range(t_new):
            rows = slice(t * nh, (t + 1) * nh)
            s = [jnp.sum(qb[t] * kn[j:j + 1, :], axis=-1, keepdims=True) * scale + (fcol[t] - fcol[j])
                 for j in range(t + 1)]
            m = s[0]
            for sj in s[1:]:
                m = jnp.maximum(m, sj)
            l = jnp.zeros((nh, 1), F32)
            acc = jnp.zeros((nh, width), F32)
            for j, sj in enumerate(s):
                p = jnp.exp(sj - m)
                l = l + p
                acc = acc + p * vn[j:j + 1, :]
            qbd_ref[rows, :] = qb[t].astype(qbd_ref.dtype)
            fn_ref[rows, :] = fcol[t]
            m_ref[rows, :] = m
            l_ref[rows, :] = l
            acc_ref[rows, :] = acc
        tail_ref[...] = jnp.zeros_like(tail_ref)

    tail = tail_ref[...]
    bias = []
    for j in range(n_pg):
        kcat_ref[j * page:(j + 1) * page, :] = k_refs[j][...].astype(kcat_ref.dtype)
        vcat_ref[j * page:(j + 1) * page, :] = v_refs[j][...].astype(vcat_ref.dtype)
        w = w_refs[j][...]
        bias.append(w[:nh, :] + tail)
        tail = tail + w[nh:, :]
    tail_ref[...] = tail
    bias = jnp.concatenate(bias, axis=1)
    s = _dot_nt(qbd_ref[...], kcat_ref[...]) * scale
    s = s + (fn_ref[...] + jnp.concatenate([bias] * t_new, axis=0))
    m_old = m_ref[...]
    m_new = jnp.maximum(m_old, jnp.max(s, axis=-1, keepdims=True))
    alpha = jnp.exp(m_old - m_new)
    p = jnp.exp(s - m_new)
    l_ref[...] = alpha * l_ref[...] + jnp.sum(p, axis=-1, keepdims=True)
    acc_ref[...] = alpha * acc_ref[...] + _dot(p.astype(BF16), vcat_ref[...])
    m_ref[...] = m_new

    @pl.when(g == pl.num_programs(1) - 1)
    def _finish():
        head_of_lane = lax.broadcasted_iota(jnp.int32, (nh, width), 1) // hd
        head_of_row = lax.broadcasted_iota(jnp.int32, (nh, width), 0)
        own = head_of_lane == head_of_row
        o = acc_ref[...] / l_ref[...]
        for t in range(t_new):
            blk = jnp.where(own, o[t * nh:(t + 1) * nh, :], 0.0)
            o_ref[t:t + 1, :] = jnp.sum(blk, axis=0, keepdims=True).astype(o_ref.dtype)


def _fox_sample(q, k, v, lf, cache_k_l, cache_v_l, wt, page_table, nbd, t_new, nh, hd):
    n_pool, page = cache_k_l.shape[:2]
    width = nh * hd
    n_pages = page_table.shape[1]
    n_pg = _pick(n_pages, (8, 4, 2, 1))
    ck = cache_k_l.reshape(n_pool, page, width)
    cv = cache_v_l.reshape(n_pool, page, width)
    rows = t_new * nh

    def page_map(j):
        return lambda b, g, pt: (pt[b, n_pages - 1 - (g * n_pg + j)], 0, 0)

    new_spec = pl.BlockSpec((None, t_new, width), lambda b, g, pt: (b, 0, 0))
    in_specs = [new_spec, new_spec, new_spec,
                pl.BlockSpec((None, t_new, LANES), lambda b, g, pt: (b, 0, 0))]
    in_specs += [pl.BlockSpec((None, page, width), page_map(j)) for j in range(n_pg)]
    in_specs += [pl.BlockSpec((None, page, width), page_map(j)) for j in range(n_pg)]
    in_specs += [pl.BlockSpec((None, 2 * nh, page), page_map(j)) for j in range(n_pg)]
    grid_spec = pltpu.PrefetchScalarGridSpec(
        num_scalar_prefetch=1,
        grid=(nbd, n_pages // n_pg),
        in_specs=in_specs,
        out_specs=pl.BlockSpec((None, t_new, width), lambda b, g, pt: (b, 0, 0)),
        scratch_shapes=[pltpu.VMEM((rows, width), BF16),
                        pltpu.VMEM((rows, 1), F32),
                        pltpu.VMEM((rows, 1), F32),
                        pltpu.VMEM((rows, 1), F32),
                        pltpu.VMEM((rows, width), F32),
                        pltpu.VMEM((nh, page), F32),
                        pltpu.VMEM((n_pg * page, width), BF16),
                        pltpu.VMEM((n_pg * page, width), BF16)])
    out = pl.pallas_call(
        functools.partial(_fox_sample_kernel, n_pg=n_pg, nh=nh, hd=hd, scale=hd ** -0.5),
        grid_spec=grid_spec,
        out_shape=jax.ShapeDtypeStruct((nbd, t_new, width), BF16),
        compiler_params=_params(2),
        name="fox_sample",
    )(page_table, q.reshape(nbd, t_new, width), k.reshape(nbd, t_new, width),
      v.reshape(nbd, t_new, width), lf.reshape(nbd, t_new, LANES),
      *([ck] * n_pg), *([cv] * n_pg), *([wt] * n_pg))
    return out.reshape(nbd * t_new, width)


def _merge_kernel(ya_ref, o_ref, woa_ref, wob_ref, sa_ref, sb_ref, m_ref):
    a = _dot(ya_ref[...], woa_ref[...])
    b = _dot(o_ref[...], wob_ref[...])
    m_ref[...] = (sa_ref[...].astype(F32) * a + sb_ref[...].astype(F32) * b).astype(m_ref.dtype)


def _merge(ya, o, woa, wob, sa, sb):
    m, ca = ya.shape
    cb = o.shape[1]
    n = woa.shape[1]
    tm = _pick(m, (1024, 512, 256, 128))
    tn = _pick(n, (512, 256, 128))
    gspec = pl.BlockSpec((tm, tn), lambda i, j: (i, j))
    return pl.pallas_call(
        _merge_kernel,
        grid=(m // tm, n // tn),
        in_specs=[pl.BlockSpec((tm, ca), lambda i, j: (i, 0)),
                  pl.BlockSpec((tm, cb), lambda i, j: (i, 0)),
                  pl.BlockSpec((ca, tn), lambda i, j: (0, j)),
                  pl.BlockSpec((cb, tn), lambda i, j: (0, j)), gspec, gspec],
        out_specs=gspec,
        out_shape=jax.ShapeDtypeStruct((m, n), BF16),
        compiler_params=_params(2),
        name="mixer_merge",
    )(ya, o, woa, wob, sa, sb)


def _oproj_kernel(mix_ref, wo_ref, x_ref, g_ref, x1_ref, xn_ref):
    x1 = x_ref[...] + _dot(mix_ref[...], wo_ref[...])
    x1_ref[...] = x1
    xn_ref[...] = _rms_scale(x1, g_ref[...]).astype(xn_ref.dtype)


def _oproj(mix, wo, x, g):
    m, d = x.shape
    tm = _pick(m, (512, 256, 128))
    rspec = pl.BlockSpec((tm, d), lambda i: (i, 0))
    return pl.pallas_call(
        _oproj_kernel,
        grid=(m // tm,),
        in_specs=[rspec, pl.BlockSpec((d, d), lambda i: (0, 0)), rspec,
                  pl.BlockSpec((1, d), lambda i: (0, 0))],
        out_specs=[rspec, rspec],
        out_shape=[jax.ShapeDtypeStruct((m, d), F32), jax.ShapeDtypeStruct((m, d), BF16)],
        compiler_params=_params(1),
        name="oproj_residual_norm",
    )(mix, wo, x, g.reshape(1, d))


def _ffn_up_prompt_kernel(xn_ref, wu_ref, wg_ref, cw_ref, h_ref, tail_ref):
    xn = xn_ref[...]
    gpre = _dot(xn, wg_ref[...])
    tpos = lax.broadcasted_iota(jnp.int32, gpre.shape, 0)
    gc = _causal_conv(gpre, cw_ref[...], tpos, None)
    h_ref[...] = (gc * _sigmoid(gc) * _dot(xn, wu_ref[...])).astype(h_ref.dtype)
    nt = tail_ref.shape[0]
    tail_ref[...] = gpre[gpre.shape[0] - nt:, :]


def _ffn_up_sample_kernel(xn_ref, wu_ref, wg_ref, cw_ref, p1_ref, p2_ref, h_ref, g_ref, *, seq):
    xn = xn_ref[...]
    gpre = _dot(xn, wg_ref[...])
    tpos = lax.broadcasted_iota(jnp.int32, gpre.shape, 0) % seq
    gc = _causal_conv(gpre, cw_ref[...], tpos, [p1_ref[...], p2_ref[...]])
    h_ref[...] = (gc * _sigmoid(gc) * _dot(xn, wu_ref[...])).astype(h_ref.dtype)
    g_ref[...] = gpre


def _ffn_up_prompt(xn, wu, wg, cw, nb, seq):
    m, d = xn.shape
    f = wu.shape[1]
    km1 = cw.shape[0] - 1
    tn = _pick(f, (256, 128))
    wspec = pl.BlockSpec((d, tn), lambda b, j: (0, j))
    return pl.pallas_call(
        _ffn_up_prompt_kernel,
        grid=(nb, f // tn),
        in_specs=[pl.BlockSpec((seq, d), lambda b, j: (b, 0)), wspec, wspec,
                  pl.BlockSpec((km1 + 1, tn), lambda b, j: (0, j))],
        out_specs=[pl.BlockSpec((seq, tn), lambda b, j: (b, j)),
                   pl.BlockSpec((None, km1, tn), lambda b, j: (b, 0, j))],
        out_shape=[jax.ShapeDtypeStruct((m, f), BF16), jax.ShapeDtypeStruct((nb, km1, f), F32)],
        compiler_params=_params(2),
        name="ffn_up_prompt",
    )(xn, wu, wg, cw)


def _ffn_up_sample(xn, wu, wg, cw, state, seq):
    m, d = xn.shape
    f = wu.shape[1]
    tn = _pick(f, (512, 256, 128))
    p1, p2 = _prev_rows(state, seq)
    wspec = pl.BlockSpec((d, tn), lambda j: (0, j))
    cspec = pl.BlockSpec((m, tn), lambda j: (0, j))
    h, gpre = pl.pallas_call(
        functools.partial(_ffn_up_sample_kernel, seq=seq),
        grid=(f // tn,),
        in_specs=[pl.BlockSpec((m, d), lambda j: (0, 0)), wspec, wspec,
                  pl.BlockSpec((cw.shape[0], tn), lambda j: (0, j)), cspec, cspec],
        out_specs=[cspec, cspec],
        out_shape=[jax.ShapeDtypeStruct((m, f), BF16), jax.ShapeDtypeStruct((m, f), F32)],
        compiler_params=_params(1),
        name="ffn_up_sample",
    )(xn, wu, wg, cw, p1, p2)
    return h, _new_state(state, gpre, seq)


def _ffn_down_kernel(h_ref, wd_ref, x1_ref, g_ref, y_ref, acc_ref, *, normalize):
    kk = pl.program_id(1)
    part = _dot(h_ref[...], wd_ref[...])

    @pl.when(kk == 0)
    def _first():
        acc_ref[...] = x1_ref[...] + part

    @pl.when(kk > 0)
    def _rest():
        acc_ref[...] = acc_ref[...] + part

    @pl.when(kk == pl.num_programs(1) - 1)
    def _last():
        x2 = acc_ref[...]
        y_ref[...] = _rms_scale(x2, g_ref[...]) if normalize else x2


def _ffn_down(h, wd, x1, g, normalize):
    m, f = h.shape
    d = x1.shape[1]
    tm = _pick(m, (512, 256, 128))
    tk = _pick(f, (1408, 1024, 512, 256, 128))
    rspec = pl.BlockSpec((tm, d), lambda i, kk: (i, 0))
    return pl.pallas_call(
        functools.partial(_ffn_down_kernel, normalize=normalize),
        grid=(m // tm, f // tk),
        in_specs=[pl.BlockSpec((tm, tk), lambda i, kk: (i, kk)),
                  pl.BlockSpec((tk, d), lambda i, kk: (kk, 0)), rspec,
                  pl.BlockSpec((1, d), lambda i, kk: (0, 0))],
        out_specs=rspec,
        out_shape=jax.ShapeDtypeStruct((m, d), F32),
        scratch_shapes=[pltpu.VMEM((tm, d), F32)],
        compiler_params=_params(2),
        name="ffn_down_residual_norm",
    )(h, wd, x1, g.reshape(1, d))


def _layer_weights(w_in_l, b_f_l, d_conv, d_attn, nh, d_model):
    sizes = (d_conv, d_conv, d_conv, d_attn, d_attn, d_attn, nh, d_model, d_model)
    offs = [0]
    for s in sizes:
        offs.append(offs[-1] + s)
    parts = [w_in_l[:, offs[i]:offs[i + 1]] for i in range(len(sizes))]
    wh, wb, wc, wq, wk, wv, wf, wga, wgb = parts
    wf_pad = jnp.pad(wf, ((0, 0), (0, LANES - nh))).astype(BF16)
    bf_pad = jnp.pad(b_f_l.astype(F32), (0, LANES - nh)).reshape(1, LANES)
    cast = lambda w: w.astype(BF16)
    return cast(wh), cast(wb), cast(wc), cast(wq), cast(wk), cast(wv), wf_pad, bf_pad, cast(wga), cast(wgb)


def kernel(x_prompt, x_sample, cache_k, cache_v, cache_lf, state_conv_a, state_conv_ffn, page_table,
           norm_mix_g, w_in, b_f, conv_a_w, w_out_a, w_out_b, w_o, norm_ffn_g, w_up, w_gate,
           conv_ffn_w, w_down, norm_final_g):
    bp, tp, d_model = x_prompt.shape
    bd, td, _ = x_sample.shape
    depth = w_in.shape[0]
    nh, hd = cache_k.shape[3], cache_k.shape[4]
    d_attn = nh * hd
    d_conv = conv_a_w.shape[2]
    d_ff = w_up.shape[2]
    assert conv_a_w.shape[1] == 3 and conv_ffn_w.shape[1] == 3
    assert nh <= SUBLANES and hd % LANES == 0

    hp = x_prompt.reshape(bp * tp, d_model)
    hs = x_sample.reshape(bd * td, d_model)
    outs = [[] for _ in range(10)]
    for l in range(depth):
        wh, wb, wc, wq, wk, wv, wf_pad, bf_pad, wga, wgb = _layer_weights(
            w_in[l], b_f[l], d_conv, d_attn, nh, d_model)
        woa, wob, wo = w_out_a[l].astype(BF16), w_out_b[l].astype(BF16), w_o[l].astype(BF16)
        wu, wg, wd = w_up[l].astype(BF16), w_gate[l].astype(BF16), w_down[l].astype(BF16)
        last = l == depth - 1

        xn = _rmsnorm_bf16(hp, norm_mix_g[l])
        ya, cap = _mixa_prompt(xn, wh, wb, wc, conv_a_w[l], bp, tp)
        q, k, v = _qkv(xn, wq, wk, wv)
        sa, sb = _gates(xn, wga, wgb)
        lf = _logforget(xn, wf_pad, bf_pad)
        fcum = _cumsum_rows(lf, bp, tp)
        o = _fox_prompt(q, k, v, fcum, bp, tp, nh, hd)
        mix = _merge(ya, o, woa, wob, sa, sb)
        x1, xn2 = _oproj(mix, wo, hp, norm_ffn_g[l])
        hact, cfp = _ffn_up_prompt(xn2, wu, wg, conv_ffn_w[l], bp, tp)
        hp = _ffn_down(hact, wd, x1, norm_final_g, last)
        kp, vp, lfp = k, v, lf[:, :nh]

        xn = _rmsnorm_bf16(hs, norm_mix_g[l])
        ya, cas = _mixa_sample(xn, wh, wb, wc, conv_a_w[l], state_conv_a[l], td)
        q, k, v = _qkv(xn, wq, wk, wv)
        sa, sb = _gates(xn, wga, wgb)
        lf = _logforget(xn, wf_pad, bf_pad)
        wt = _page_suffix(cache_lf[l])
        o = _fox_sample(q, k, v, lf, cache_k[l], cache_v[l], wt, page_table, bd, td, nh, hd)
        mix = _merge(ya, o, woa, wob, sa, sb)
        x1s, xn2 = _oproj(mix, wo, hs, norm_ffn_g[l])
        hact, cfs = _ffn_up_sample(xn2, wu, wg, conv_ffn_w[l], state_conv_ffn[l], td)
        hs = _ffn_down(hact, wd, x1s, norm_final_g, last)

        for lst, val in zip(outs, (kp.reshape(bp, tp, nh, hd), vp.reshape(bp, tp, nh, hd),
                                   lfp.reshape(bp, tp, nh), k.reshape(bd, td, nh, hd),
                                   v.reshape(bd, td, nh, hd), lf[:, :nh].reshape(bd, td, nh),
                                   cap, cas, cfp, cfs)):
            lst.append(val)
    return (hp.reshape(bp, tp, d_model), hs.reshape(bd, td, d_model), *[jnp.stack(o_) for o_ in outs])
```

```python
import functools
from typing import NamedTuple

import jax
import jax.numpy as jnp
from jax import lax
from jax.experimental import pallas as pl
from jax.experimental.pallas import tpu as pltpu

F32 = jnp.float32
BF16 = jnp.bfloat16
EPS = 1e-6
LANES = 128
SUBLANES = 8
VMEM_LIMIT = 56 * 1024 * 1024
NEG_BIG = -1e30
LOG2E = 1.4426950408889634


class _Cols(NamedTuple):
    array: jax.Array
    start: int
    width: int


def _col_operand(cols, tn):
    if cols.start % tn == 0:
        return cols.array, cols.start // tn
    return cols.array[:, cols.start:cols.start + cols.width], 0


def _pick(n, cands):
    for c in cands:
        if n % c == 0:
            return c
    return n


def _params(n_axes):
    return pltpu.CompilerParams(dimension_semantics=("arbitrary",) * n_axes,
                                vmem_limit_bytes=VMEM_LIMIT)


def _dot(a, b):
    return jnp.dot(a, b, preferred_element_type=F32)


def _dot_nt(a, b):
    return lax.dot_general(a, b, (((1,), (1,)), ((), ())), preferred_element_type=F32)


def _split3(x):
    hi = x.astype(BF16)
    r1 = x - hi.astype(F32)
    mid = r1.astype(BF16)
    lo = (r1 - mid.astype(F32)).astype(BF16)
    return hi, mid, lo


def _dot01(ones_mat, x, ones_on_left):
    out = None
    for part in _split3(x):
        t = _dot(ones_mat, part) if ones_on_left else _dot(part, ones_mat)
        out = t if out is None else out + t
    return out


def _sigmoid(x):
    return 1.0 / (1.0 + jnp.exp(-x))


def _log_sigmoid(x):
    return jnp.minimum(x, 0.0) - jnp.log1p(jnp.exp(-jnp.abs(x)))


def _rms_scale(x, g):
    var = jnp.mean(x * x, axis=-1, keepdims=True)
    return x * lax.rsqrt(var + EPS) * g


def _rmsnorm_kernel(x_ref, g_ref, o_ref):
    o_ref[...] = _rms_scale(x_ref[...], g_ref[...]).astype(o_ref.dtype)


def _rmsnorm_bf16(x, g):
    m, d = x.shape
    tm = _pick(m, (512, 256, 128))
    return pl.pallas_call(
        _rmsnorm_kernel,
        grid=(m // tm,),
        in_specs=[pl.BlockSpec((tm, d), lambda i: (i, 0)),
                  pl.BlockSpec((1, d), lambda i: (0, 0))],
        out_specs=pl.BlockSpec((tm, d), lambda i: (i, 0)),
        out_shape=jax.ShapeDtypeStruct((m, d), BF16),
        compiler_params=_params(1),
        name="rmsnorm_bf16",
    )(x, g.reshape(1, d))


def _causal_conv(u, cw, tpos, prev):
    k = cw.shape[0]
    y = u * cw[k - 1:k, :]
    for j in range(1, k):
        shifted = pltpu.roll(u, j, axis=0)
        fill = 0.0 if prev is None else prev[j - 1]
        y = y + jnp.where(tpos >= j, shifted, fill) * cw[k - 1 - j:k - j, :]
    return y


def _mixa_prompt_kernel(xn_ref, wh_ref, wb_ref, wc_ref, cw_ref, ya_ref, tail_ref):
    xn = xn_ref[...]
    u = _dot(xn, wc_ref[...]) * _dot(xn, wh_ref[...])
    tpos = lax.broadcasted_iota(jnp.int32, u.shape, 0)
    y = _causal_conv(u, cw_ref[...], tpos, None)
    ya_ref[...] = (_dot(xn, wb_ref[...]) * y).astype(ya_ref.dtype)
    nt = tail_ref.shape[0]
    tail_ref[...] = u[u.shape[0] - nt:, :]


def _mixa_sample_kernel(xn_ref, wh_ref, wb_ref, wc_ref, cw_ref, p1_ref, p2_ref, ya_ref, u_ref, *, seq):
    xn = xn_ref[...]
    u = _dot(xn, wc_ref[...]) * _dot(xn, wh_ref[...])
    tpos = lax.broadcasted_iota(jnp.int32, u.shape, 0) % seq
    y = _causal_conv(u, cw_ref[...], tpos, [p1_ref[...], p2_ref[...]])
    ya_ref[...] = (_dot(xn, wb_ref[...]) * y).astype(ya_ref.dtype)
    u_ref[...] = u


def _prev_rows(state, seq):
    b, km1, c = state.shape
    outs = []
    for j in range(1, km1 + 1):
        rows = [state[:, km1 - j + t, :] if t < j else jnp.zeros((b, c), state.dtype) for t in range(seq)]
        outs.append(jnp.stack(rows, axis=1).reshape(b * seq, c))
    return outs


def _new_state(state, u, seq):
    b, km1, c = state.shape
    ext = jnp.concatenate([state, u.reshape(b, seq, c)], axis=1)
    return ext[:, -km1:, :]


def _mixa_prompt(xn, wh, wb, wc, cw, nb, seq):
    m, d = xn.shape
    c = wh.width
    km1 = cw.shape[0] - 1
    tn = _pick(c, (256, 128))
    (wh, oh), (wb, ob), (wc, oc) = (_col_operand(w, tn) for w in (wh, wb, wc))
    wspec = lambda off: pl.BlockSpec((d, tn), lambda b, j: (0, j + off))
    return pl.pallas_call(
        _mixa_prompt_kernel,
        grid=(nb, c // tn),
        in_specs=[pl.BlockSpec((seq, d), lambda b, j: (b, 0)), wspec(oh), wspec(ob), wspec(oc),
                  pl.BlockSpec((km1 + 1, tn), lambda b, j: (0, j))],
        out_specs=[pl.BlockSpec((seq, tn), lambda b, j: (b, j)),
                   pl.BlockSpec((None, km1, tn), lambda b, j: (b, 0, j))],
        out_shape=[jax.ShapeDtypeStruct((m, c), BF16),
                   jax.ShapeDtypeStruct((nb, km1, c), F32)],
        compiler_params=_params(2),
        name="mixa_prompt",
    )(xn, wh, wb, wc, cw)


def _mixa_sample(xn, wh, wb, wc, cw, state, seq):
    m, d = xn.shape
    c = wh.width
    tn = _pick(c, (512, 256, 128))
    p1, p2 = _prev_rows(state, seq)
    (wh, oh), (wb, ob), (wc, oc) = (_col_operand(w, tn) for w in (wh, wb, wc))
    wspec = lambda off: pl.BlockSpec((d, tn), lambda j: (0, j + off))
    cspec = pl.BlockSpec((m, tn), lambda j: (0, j))
    ya, u = pl.pallas_call(
        functools.partial(_mixa_sample_kernel, seq=seq),
        grid=(c // tn,),
        in_specs=[pl.BlockSpec((m, d), lambda j: (0, 0)), wspec(oh), wspec(ob), wspec(oc),
                  pl.BlockSpec((cw.shape[0], tn), lambda j: (0, j)), cspec, cspec],
        out_specs=[cspec, cspec],
        out_shape=[jax.ShapeDtypeStruct((m, c), BF16), jax.ShapeDtypeStruct((m, c), F32)],
        compiler_params=_params(1),
        name="mixa_sample",
    )(xn, wh, wb, wc, cw, p1, p2)
    return ya, _new_state(state, u, seq)


def _qkv_kernel(xn_ref, wq_ref, wk_ref, wv_ref, q_ref, k_ref, v_ref, *, q_scale):
    xn = xn_ref[...]
    q_ref[...] = (_dot(xn, wq_ref[...]) * q_scale).astype(q_ref.dtype)
    k_ref[...] = _dot(xn, wk_ref[...])
    v_ref[...] = _dot(xn, wv_ref[...])


def _qkv(xn, wq, wk, wv, q_scale):
    m, d = xn.shape
    n = wq.width
    tm = _pick(m, (1024, 512, 256, 128))
    tn = _pick(n, (512, 256, 128))
    (wq, oq), (wk, ok), (wv, ov) = (_col_operand(w, tn) for w in (wq, wk, wv))
    wspec = lambda off: pl.BlockSpec((d, tn), lambda i, j: (0, j + off))
    ospec = pl.BlockSpec((tm, tn), lambda i, j: (i, j))
    return pl.pallas_call(
        functools.partial(_qkv_kernel, q_scale=q_scale),
        grid=(m // tm, n // tn),
        in_specs=[pl.BlockSpec((tm, d), lambda i, j: (i, 0)), wspec(oq), wspec(ok), wspec(ov)],
        out_specs=[ospec, ospec, ospec],
        out_shape=[jax.ShapeDtypeStruct((m, n), BF16), jax.ShapeDtypeStruct((m, n), F32),
                   jax.ShapeDtypeStruct((m, n), F32)],
        compiler_params=_params(2),
        name="qkv_proj",
    )(xn, wq, wk, wv)


def _gates_kernel(xn_ref, wa_ref, wb_ref, wf_ref, bf_ref, sa_ref, sb_ref, lf_ref):
    xn = xn_ref[...]
    sa_ref[...] = _sigmoid(_dot(xn, wa_ref[...])).astype(sa_ref.dtype)
    sb_ref[...] = _sigmoid(_dot(xn, wb_ref[...])).astype(sb_ref.dtype)

    @pl.when(pl.program_id(1) == 0)
    def _logforget():
        lf_ref[...] = _log_sigmoid(_dot(xn, wf_ref[...]) + bf_ref[...])


def _gates(xn, wa, wb, wf_pad, bf_pad):
    m, d = xn.shape
    n = wa.width
    tm = _pick(m, (2048, 1024, 512, 256, 128))
    tn = _pick(n, (512, 256, 128))
    (wa, oa), (wb, ob) = (_col_operand(w, tn) for w in (wa, wb))
    wspec = lambda off: pl.BlockSpec((d, tn), lambda i, j: (0, j + off))
    ospec = pl.BlockSpec((tm, tn), lambda i, j: (i, j))
    return pl.pallas_call(
        _gates_kernel,
        grid=(m // tm, n // tn),
        in_specs=[pl.BlockSpec((tm, d), lambda i, j: (i, 0)), wspec(oa), wspec(ob),
                  pl.BlockSpec((d, LANES), lambda i, j: (0, 0)),
                  pl.BlockSpec((1, LANES), lambda i, j: (0, 0))],
        out_specs=[ospec, ospec, pl.BlockSpec((tm, LANES), lambda i, j: (i, 0))],
        out_shape=[jax.ShapeDtypeStruct((m, n), BF16), jax.ShapeDtypeStruct((m, n), BF16),
                   jax.ShapeDtypeStruct((m, LANES), F32)],
        compiler_params=_params(2),
        name="gate_proj",
    )(xn, wa, wb, wf_pad, bf_pad)


def _cumsum_kernel(lf_ref, f_ref, *, blk):
    t = lf_ref.shape[0]
    r = lax.broadcasted_iota(jnp.int32, (blk, blk), 0)
    c = lax.broadcasted_iota(jnp.int32, (blk, blk), 1)
    lower = (c <= r).astype(BF16)
    carry = jnp.zeros((1, lf_ref.shape[1]), F32)
    for i in range(t // blk):
        y = _dot01(lower, lf_ref[i * blk:(i + 1) * blk, :], True) + carry
        f_ref[i * blk:(i + 1) * blk, :] = y
        carry = y[blk - 1:blk, :]


def _cumsum_rows(lf, nb, seq):
    blk = _pick(seq, (256, 128))
    return pl.pallas_call(
        functools.partial(_cumsum_kernel, blk=blk),
        grid=(nb,),
        in_specs=[pl.BlockSpec((seq, LANES), lambda b: (b, 0))],
        out_specs=pl.BlockSpec((seq, LANES), lambda b: (b, 0)),
        out_shape=jax.ShapeDtypeStruct(lf.shape, F32),
        compiler_params=_params(1),
        name="logforget_cumsum",
    )(lf)


def _fox_prompt_kernel(q_ref, k_ref, v_ref, fcol_ref, frow_ref, o_ref):
    h = pl.program_id(1)
    i = pl.program_id(2)
    tq, hd = q_ref.shape
    q = q_ref[...]
    lane = lax.broadcasted_iota(jnp.int32, fcol_ref.shape, 1)
    fq = jnp.sum(jnp.where(lane == h, fcol_ref[...], 0.0), axis=-1, keepdims=True) * LOG2E

    def block(kb, carry, diagonal):
        m, l, acc = carry
        off = pl.multiple_of(kb * tq, tq)
        kblk = k_ref[pl.ds(off, tq), :].astype(BF16)
        t = _dot_nt(q, kblk) - frow_ref[:, pl.ds(off, tq)] * LOG2E
        if diagonal:
            rows = lax.broadcasted_iota(jnp.int32, t.shape, 0)
            cols = lax.broadcasted_iota(jnp.int32, t.shape, 1)
            t = jnp.where(cols <= rows, t, -jnp.inf)
        m_new = jnp.maximum(m, jnp.max(t, axis=-1, keepdims=True) + fq)
        alpha = jnp.exp2(m - m_new)
        p = jnp.exp2(t - (m_new - fq))
        l = alpha * l + jnp.sum(p, axis=-1, keepdims=True)
        acc = alpha * acc + _dot(p.astype(BF16), v_ref[pl.ds(off, tq), :].astype(BF16))
        return m_new, l, acc

    init = (jnp.full((tq, 1), NEG_BIG, F32), jnp.zeros((tq, 1), F32), jnp.zeros((tq, hd), F32))
    carry = block(i, init, True)
    _, l, acc = lax.fori_loop(0, i, lambda kb, c: block(kb, c, False), carry)
    o_ref[...] = (acc / l).astype(o_ref.dtype)


def _fox_prompt(q, k, v, fcum, nb, seq, nh, hd):
    m = q.shape[0]
    tq = _pick(seq, (512, 256, 128))
    nq = seq // tq
    frow = fcum[:, :nh].reshape(nb, seq, nh).transpose(0, 2, 1).reshape(nb * nh, 1, seq)
    return pl.pallas_call(
        _fox_prompt_kernel,
        grid=(nb, nh, nq),
        in_specs=[pl.BlockSpec((tq, hd), lambda b, h, i: (b * nq + i, h)),
                  pl.BlockSpec((seq, hd), lambda b, h, i: (b, h)),
                  pl.BlockSpec((seq, hd), lambda b, h, i: (b, h)),
                  pl.BlockSpec((tq, LANES), lambda b, h, i: (b * nq + i, 0)),
                  pl.BlockSpec((None, 1, seq), lambda b, h, i: (b * nh + h, 0, 0))],
        out_specs=pl.BlockSpec((tq, hd), lambda b, h, i: (b * nq + i, h)),
        out_shape=jax.ShapeDtypeStruct((m, nh * hd), BF16),
        compiler_params=_params(3),
        name="fox_prompt",
    )(q, k, v, fcum, frow)


def _suffix_kernel(lft_ref, o_ref):
    rows, page = lft_ref.shape
    r = lax.broadcasted_iota(jnp.int32, (page, 2 * page), 0)
    c = lax.broadcasted_iota(jnp.int32, (page, 2 * page), 1)
    sel = jnp.logical_or(c >= page, r > c).astype(BF16)
    y = _dot01(sel, lft_ref[...], False)
    nh = o_ref.shape[1] // 2
    o_ref[:, :nh, :] = y[:, :page].reshape(rows // nh, nh, page)
    o_ref[:, nh:, :] = y[:, page:].reshape(rows // nh, nh, page)


def _page_suffix(cache_lf_l):
    n_pool, page, nh = cache_lf_l.shape
    lft = cache_lf_l.transpose(0, 2, 1).reshape(n_pool * nh, page)
    pp = _pick(n_pool, (512, 256, 128, 64, 32, 16, 8, 4, 2, 1))
    return pl.pallas_call(
        _suffix_kernel,
        grid=(n_pool // pp,),
        in_specs=[pl.BlockSpec((pp * nh, page), lambda i: (i, 0))],
        out_specs=pl.BlockSpec((pp, 2 * nh, page), lambda i: (i, 0, 0)),
        out_shape=jax.ShapeDtypeStruct((n_pool, 2 * nh, page), F32),
        compiler_params=_params(1),
        name="page_suffix",
    )(lft)


def _fox_sample_kernel(pt_ref, q_ref, kn_ref, vn_ref, lfn_ref, *rest, n_pg, nh, hd):
    del pt_ref
    k_refs = rest[:n_pg]
    v_refs = rest[n_pg:2 * n_pg]
    w_refs = rest[2 * n_pg:3 * n_pg]
    o_ref = rest[3 * n_pg]
    qbd_ref, fn_ref, m_ref, l_ref, acc_ref, tail_ref, kcat_ref, vcat_ref = rest[3 * n_pg + 1:]
    g = pl.program_id(1)
    t_new, width = q_ref.shape
    page = k_refs[0].shape[0]

    @pl.when(g == 0)
    def _init():
        head_of_lane = lax.broadcasted_iota(jnp.int32, (nh, width), 1) // hd
        head_of_row = lax.broadcasted_iota(jnp.int32, (nh, width), 0)
        own = head_of_lane == head_of_row
        eye = (lax.broadcasted_iota(jnp.int32, (nh, LANES), 0)
               == lax.broadcasted_iota(jnp.int32, (nh, LANES), 1))
        qf = q_ref[...].astype(F32)
        kn = kn_ref[...]
        vn = vn_ref[...]
        lfn = lfn_ref[...]
        qb, fcol = [], []
        run = jnp.zeros((1, LANES), F32)
        for t in range(t_new):
            qb.append(jnp.where(own, jnp.broadcast_to(qf[t:t + 1, :], (nh, width)), 0.0))
            run = run + lfn[t:t + 1, :]
            fcol.append(jnp.sum(jnp.where(eye, jnp.broadcast_to(run, (nh, LANES)), 0.0),
                                axis=-1, keepdims=True) * LOG2E)
        for t in range(t_new):
            rows = slice(t * nh, (t + 1) * nh)
            s = [jnp.sum(qb[t] * kn[j:j + 1, :], axis=-1, keepdims=True) + (fcol[t] - fcol[j])
                 for j in range(t + 1)]
            m = s[0]
            for sj in s[1:]:
                m = jnp.maximum(m, sj)
            l = jnp.zeros((nh, 1), F32)
            acc = jnp.zeros((nh, width), F32)
            for j, sj in enumerate(s):
                p = jnp.exp2(sj - m)
                l = l + p
                acc = acc + p * vn[j:j + 1, :]
            qbd_ref[rows, :] = qb[t].astype(qbd_ref.dtype)
            fn_ref[rows, :] = fcol[t]
            m_ref[rows, :] = m
            l_ref[rows, :] = l
            acc_ref[rows, :] = acc
        tail_ref[...] = jnp.zeros_like(tail_ref)

    tail = tail_ref[...]
    bias = []
    for j in range(n_pg):
        kcat_ref[j * page:(j + 1) * page, :] = k_refs[j][...].astype(kcat_ref.dtype)
        vcat_ref[j * page:(j + 1) * page, :] = v_refs[j][...].astype(vcat_ref.dtype)
        w = w_refs[j][...]
        bias.append(w[:nh, :] + tail)
        tail = tail + w[nh:, :]
    tail_ref[...] = tail
    bias = jnp.concatenate(bias, axis=1) * LOG2E
    s = _dot_nt(qbd_ref[...], kcat_ref[...])
    s = s + (fn_ref[...] + jnp.concatenate([bias] * t_new, axis=0))
    m_old = m_ref[...]
    m_new = jnp.maximum(m_old, jnp.max(s, axis=-1, keepdims=True))
    alpha = jnp.exp2(m_old - m_new)
    p = jnp.exp2(s - m_new)
    l_ref[...] = alpha * l_ref[...] + jnp.sum(p, axis=-1, keepdims=True)
    acc_ref[...] = alpha * acc_ref[...] + _dot(p.astype(BF16), vcat_ref[...])
    m_ref[...] = m_new

    @pl.when(g == pl.num_programs(1) - 1)
    def _finish():
        head_of_lane = lax.broadcasted_iota(jnp.int32, (nh, width), 1) // hd
        head_of_row = lax.broadcasted_iota(jnp.int32, (nh, width), 0)
        own = head_of_lane == head_of_row
        o = acc_ref[...] / l_ref[...]
        for t in range(t_new):
            blk = jnp.where(own, o[t * nh:(t + 1) * nh, :], 0.0)
            o_ref[t:t + 1, :] = jnp.sum(blk, axis=0, keepdims=True).astype(o_ref.dtype)


def _fox_sample(q, k, v, lf, cache_k, cache_v, layer, wt, page_table, nbd, t_new, nh, hd):
    depth, n_pool, page = cache_k.shape[:3]
    width = nh * hd
    n_pages = page_table.shape[1]
    n_pg = _pick(n_pages, (16, 8, 4, 2, 1))
    ck = cache_k.reshape(depth * n_pool, page, width)
    cv = cache_v.reshape(depth * n_pool, page, width)
    rows = t_new * nh
    base = layer * n_pool

    def page_map(j, offset):
        return lambda b, g, pt: (pt[b, n_pages - 1 - (g * n_pg + j)] + offset, 0, 0)

    new_spec = pl.BlockSpec((None, t_new, width), lambda b, g, pt: (b, 0, 0))
    in_specs = [new_spec, new_spec, new_spec,
                pl.BlockSpec((None, t_new, LANES), lambda b, g, pt: (b, 0, 0))]
    in_specs += [pl.BlockSpec((None, page, width), page_map(j, base)) for j in range(n_pg)]
    in_specs += [pl.BlockSpec((None, page, width), page_map(j, base)) for j in range(n_pg)]
    in_specs += [pl.BlockSpec((None, 2 * nh, page), page_map(j, 0)) for j in range(n_pg)]
    grid_spec = pltpu.PrefetchScalarGridSpec(
        num_scalar_prefetch=1,
        grid=(nbd, n_pages // n_pg),
        in_specs=in_specs,
        out_specs=pl.BlockSpec((None, t_new, width), lambda b, g, pt: (b, 0, 0)),
        scratch_shapes=[pltpu.VMEM((rows, width), BF16),
                        pltpu.VMEM((rows, 1), F32),
                        pltpu.VMEM((rows, 1), F32),
                        pltpu.VMEM((rows, 1), F32),
                        pltpu.VMEM((rows, width), F32),
                        pltpu.VMEM((nh, page), F32),
                        pltpu.VMEM((n_pg * page, width), BF16),
                        pltpu.VMEM((n_pg * page, width), BF16)])
    out = pl.pallas_call(
        functools.partial(_fox_sample_kernel, n_pg=n_pg, nh=nh, hd=hd),
        grid_spec=grid_spec,
        out_shape=jax.ShapeDtypeStruct((nbd, t_new, width), BF16),
        compiler_params=_params(2),
        name="fox_sample",
    )(page_table, q.reshape(nbd, t_new, width), k.reshape(nbd, t_new, width),
      v.reshape(nbd, t_new, width), lf.reshape(nbd, t_new, LANES),
      *([ck] * n_pg), *([cv] * n_pg), *([wt] * n_pg))
    return out.reshape(nbd * t_new, width)


def _merge_kernel(ya_ref, o_ref, woa_ref, wob_ref, sa_ref, sb_ref, m_ref):
    a = _dot(ya_ref[...], woa_ref[...])
    b = _dot(o_ref[...], wob_ref[...])
    m_ref[...] = (sa_ref[...].astype(F32) * a + sb_ref[...].astype(F32) * b).astype(m_ref.dtype)


def _merge(ya, o, woa, wob, sa, sb):
    m, ca = ya.shape
    cb = o.shape[1]
    n = woa.shape[1]
    tm = _pick(m, (2048, 1024, 512, 256, 128))
    tn = _pick(n, (512, 256, 128))
    gspec = pl.BlockSpec((tm, tn), lambda i, j: (i, j))
    return pl.pallas_call(
        _merge_kernel,
        grid=(m // tm, n // tn),
        in_specs=[pl.BlockSpec((tm, ca), lambda i, j: (i, 0)),
                  pl.BlockSpec((tm, cb), lambda i, j: (i, 0)),
                  pl.BlockSpec((ca, tn), lambda i, j: (0, j)),
                  pl.BlockSpec((cb, tn), lambda i, j: (0, j)), gspec, gspec],
        out_specs=gspec,
        out_shape=jax.ShapeDtypeStruct((m, n), BF16),
        compiler_params=_params(2),
        name="mixer_merge",
    )(ya, o, woa, wob, sa, sb)


def _oproj_kernel(mix_ref, wo_ref, x_ref, g_ref, x1_ref, xn_ref):
    x1 = x_ref[...] + _dot(mix_ref[...], wo_ref[...])
    x1_ref[...] = x1
    xn_ref[...] = _rms_scale(x1, g_ref[...]).astype(xn_ref.dtype)


def _oproj(mix, wo, x, g):
    m, d = x.shape
    tm = _pick(m, (512, 256, 128))
    rspec = pl.BlockSpec((tm, d), lambda i: (i, 0))
    return pl.pallas_call(
        _oproj_kernel,
        grid=(m // tm,),
        in_specs=[rspec, pl.BlockSpec((d, d), lambda i: (0, 0)), rspec,
                  pl.BlockSpec((1, d), lambda i: (0, 0))],
        out_specs=[rspec, rspec],
        out_shape=[jax.ShapeDtypeStruct((m, d), F32), jax.ShapeDtypeStruct((m, d), BF16)],
        compiler_params=_params(1),
        name="oproj_residual_norm",
    )(mix, wo, x, g.reshape(1, d))


def _ffn_up_prompt_kernel(xn_ref, wu_ref, wg_ref, cw_ref, h_ref, tail_ref):
    xn = xn_ref[...]
    gpre = _dot(xn, wg_ref[...])
    tpos = lax.broadcasted_iota(jnp.int32, gpre.shape, 0)
    gc = _causal_conv(gpre, cw_ref[...], tpos, None)
    h_ref[...] = (gc * _sigmoid(gc) * _dot(xn, wu_ref[...])).astype(h_ref.dtype)
    nt = tail_ref.shape[0]
    tail_ref[...] = gpre[gpre.shape[0] - nt:, :]


def _ffn_up_sample_kernel(xn_ref, wu_ref, wg_ref, cw_ref, p1_ref, p2_ref, h_ref, g_ref, *, seq):
    xn = xn_ref[...]
    gpre = _dot(xn, wg_ref[...])
    tpos = lax.broadcasted_iota(jnp.int32, gpre.shape, 0) % seq
    gc = _causal_conv(gpre, cw_ref[...], tpos, [p1_ref[...], p2_ref[...]])
    h_ref[...] = (gc * _sigmoid(gc) * _dot(xn, wu_ref[...])).astype(h_ref.dtype)
    g_ref[...] = gpre


def _ffn_up_prompt(xn, wu, wg, cw, nb, seq):
    m, d = xn.shape
    f = wu.shape[1]
    km1 = cw.shape[0] - 1
    tn = _pick(f, (512, 256, 128))
    wspec = pl.BlockSpec((d, tn), lambda b, j: (0, j))
    return pl.pallas_call(
        _ffn_up_prompt_kernel,
        grid=(nb, f // tn),
        in_specs=[pl.BlockSpec((seq, d), lambda b, j: (b, 0)), wspec, wspec,
                  pl.BlockSpec((km1 + 1, tn), lambda b, j: (0, j))],
        out_specs=[pl.BlockSpec((seq, tn), lambda b, j: (b, j)),
                   pl.BlockSpec((None, km1, tn), lambda b, j: (b, 0, j))],
        out_shape=[jax.ShapeDtypeStruct((m, f), BF16), jax.ShapeDtypeStruct((nb, km1, f), F32)],
        compiler_params=_params(2),
        name="ffn_up_prompt",
    )(xn, wu, wg, cw)


def _ffn_up_sample(xn, wu, wg, cw, state, seq):
    m, d = xn.shape
    f = wu.shape[1]
    tn = _pick(f, (512, 256, 128))
    p1, p2 = _prev_rows(state, seq)
    wspec = pl.BlockSpec((d, tn), lambda j: (0, j))
    cspec = pl.BlockSpec((m, tn), lambda j: (0, j))
    h, gpre = pl.pallas_call(
        functools.partial(_ffn_up_sample_kernel, seq=seq),
        grid=(f // tn,),
        in_specs=[pl.BlockSpec((m, d), lambda j: (0, 0)), wspec, wspec,
                  pl.BlockSpec((cw.shape[0], tn), lambda j: (0, j)), cspec, cspec],
        out_specs=[cspec, cspec],
        out_shape=[jax.ShapeDtypeStruct((m, f), BF16), jax.ShapeDtypeStruct((m, f), F32)],
        compiler_params=_params(1),
        name="ffn_up_sample",
    )(xn, wu, wg, cw, p1, p2)
    return h, _new_state(state, gpre, seq)


def _ffn_down_kernel(h_ref, wd_ref, x1_ref, g_ref, y_ref, *, normalize, nk):
    kk = pl.program_id(1)
    part = _dot(h_ref[...], wd_ref[...])
    finish = (lambda x2: _rms_scale(x2, g_ref[...])) if normalize else (lambda x2: x2)
    if nk == 1:
        y_ref[...] = finish(x1_ref[...] + part)
        return

    @pl.when(kk == 0)
    def _first():
        y_ref[...] = x1_ref[...] + part

    @pl.when(jnp.logical_and(kk > 0, kk < nk - 1))
    def _middle():
        y_ref[...] = y_ref[...] + part

    @pl.when(kk == nk - 1)
    def _last():
        y_ref[...] = finish(y_ref[...] + part)


def _ffn_down(h, wd, x1, g, normalize):
    m, f = h.shape
    d = x1.shape[1]
    tm = _pick(m, (1024, 512, 256, 128))
    tk = _pick(f, (1408, 1024, 512, 256, 128))
    return pl.pallas_call(
        functools.partial(_ffn_down_kernel, normalize=normalize, nk=f // tk),
        grid=(m // tm, f // tk),
        in_specs=[pl.BlockSpec((tm, tk), lambda i, kk: (i, kk)),
                  pl.BlockSpec((tk, d), lambda i, kk: (kk, 0)),
                  pl.BlockSpec((tm, d), lambda i, kk: (i, 0), pipeline_mode=pl.Buffered(1)),
                  pl.BlockSpec((1, d), lambda i, kk: (0, 0))],
        out_specs=pl.BlockSpec((tm, d), lambda i, kk: (i, 0)),
        out_shape=jax.ShapeDtypeStruct((m, d), F32),
        compiler_params=_params(2),
        name="ffn_down_residual_norm",
    )(h, wd, x1, g.reshape(1, d))


def _in_proj_weights(w_in_l, b_f_l, d_conv, d_attn, nh, d_model):
    w = w_in_l.astype(BF16)
    sizes = (d_conv, d_conv, d_conv, d_attn, d_attn, d_attn, nh, d_model, d_model)
    cols, start = [], 0
    for s in sizes:
        cols.append(_Cols(w, start, s))
        start += s
    wf = cols[6]
    wf_pad = jnp.pad(w[:, wf.start:wf.start + nh], ((0, 0), (0, LANES - nh)))
    bf_pad = jnp.pad(b_f_l.astype(F32), (0, LANES - nh)).reshape(1, LANES)
    return cols[:6], cols[7:], wf_pad, bf_pad


def kernel(x_prompt, x_sample, cache_k, cache_v, cache_lf, state_conv_a, state_conv_ffn, page_table,
           norm_mix_g, w_in, b_f, conv_a_w, w_out_a, w_out_b, w_o, norm_ffn_g, w_up, w_gate,
           conv_ffn_w, w_down, norm_final_g):
    bp, tp, d_model = x_prompt.shape
    bd, td, _ = x_sample.shape
    depth = w_in.shape[0]
    nh, hd = cache_k.shape[3], cache_k.shape[4]
    d_attn = nh * hd
    d_conv = conv_a_w.shape[2]
    assert conv_a_w.shape[1] == 3 and conv_ffn_w.shape[1] == 3
    assert nh <= SUBLANES and hd % LANES == 0
    q_scale = hd ** -0.5 * LOG2E

    hp = x_prompt.reshape(bp * tp, d_model)
    hs = x_sample.reshape(bd * td, d_model)
    outs = [[] for _ in range(10)]
    for l in range(depth):
        (wh, wb, wc, wq, wk, wv), (wga, wgb), wf_pad, bf_pad = _in_proj_weights(
            w_in[l], b_f[l], d_conv, d_attn, nh, d_model)
        woa, wob, wo = w_out_a[l].astype(BF16), w_out_b[l].astype(BF16), w_o[l].astype(BF16)
        wu, wg, wd = w_up[l].astype(BF16), w_gate[l].astype(BF16), w_down[l].astype(BF16)
        last = l == depth - 1

        xn = _rmsnorm_bf16(hp, norm_mix_g[l])
        ya, cap = _mixa_prompt(xn, wh, wb, wc, conv_a_w[l], bp, tp)
        q, kp, vp = _qkv(xn, wq, wk, wv, q_scale)
        sa, sb, lfp = _gates(xn, wga, wgb, wf_pad, bf_pad)
        fcum = _cumsum_rows(lfp, bp, tp)
        o = _fox_prompt(q, kp, vp, fcum, bp, tp, nh, hd)
        mix = _merge(ya, o, woa, wob, sa, sb)
        x1, xn2 = _oproj(mix, wo, hp, norm_ffn_g[l])
        hact, cfp = _ffn_up_prompt(xn2, wu, wg, conv_ffn_w[l], bp, tp)
        hp = _ffn_down(hact, wd, x1, norm_final_g, last)

        xn = _rmsnorm_bf16(hs, norm_mix_g[l])
        ya, cas = _mixa_sample(xn, wh, wb, wc, conv_a_w[l], state_conv_a[l], td)
        q, ks, vs = _qkv(xn, wq, wk, wv, q_scale)
        sa, sb, lfs = _gates(xn, wga, wgb, wf_pad, bf_pad)
        wt = _page_suffix(cache_lf[l])
        o = _fox_sample(q, ks, vs, lfs, cache_k, cache_v, l, wt, page_table, bd, td, nh, hd)
        mix = _merge(ya, o, woa, wob, sa, sb)
        x1s, xn2 = _oproj(mix, wo, hs, norm_ffn_g[l])
        hact, cfs = _ffn_up_sample(xn2, wu, wg, conv_ffn_w[l], state_conv_ffn[l], td)
        hs = _ffn_down(hact, wd, x1s, norm_final_g, last)

        for lst, val in zip(outs, (kp.reshape(bp, tp, nh, hd), vp.reshape(bp, tp, nh, hd),
                                   lfp[:, :nh].reshape(bp, tp, nh), ks.reshape(bd, td, nh, hd),
                                   vs.reshape(bd, td, nh, hd), lfs[:, :nh].reshape(bd, td, nh),
                                   cap, cas, cfp, cfs)):
            lst.append(val)
    return (hp.reshape(bp, tp, d_model), hs.reshape(bd, td, d_model), *[jnp.stack(o_) for o_ in outs])
```

```python
import functools
from typing import NamedTuple

import jax
import jax.numpy as jnp
from jax import lax
from jax.experimental import pallas as pl
from jax.experimental.pallas import tpu as pltpu

F32 = jnp.float32
BF16 = jnp.bfloat16
EPS = 1e-6
LANES = 128
SUBLANES = 8
VMEM_LIMIT = 56 * 1024 * 1024
NEG_BIG = -1e30
LOG2E = 1.4426950408889634


class _Cols(NamedTuple):
    array: jax.Array
    start: int
    width: int


def _col_operand(cols, tn):
    if cols.start % tn == 0:
        return cols.array, cols.start // tn
    return cols.array[:, cols.start:cols.start + cols.width], 0


def _pick(n, cands):
    for c in cands:
        if n % c == 0:
            return c
    return n


def _params(n_axes):
    return pltpu.CompilerParams(dimension_semantics=("arbitrary",) * n_axes,
                                vmem_limit_bytes=VMEM_LIMIT)


def _dot(a, b):
    return jnp.dot(a, b, preferred_element_type=F32)


def _dot_nt(a, b):
    return lax.dot_general(a, b, (((1,), (1,)), ((), ())), preferred_element_type=F32)


def _split3(x):
    hi = x.astype(BF16)
    r1 = x - hi.astype(F32)
    mid = r1.astype(BF16)
    lo = (r1 - mid.astype(F32)).astype(BF16)
    return hi, mid, lo


def _dot01(ones_mat, x, ones_on_left):
    out = None
    for part in _split3(x):
        t = _dot(ones_mat, part) if ones_on_left else _dot(part, ones_mat)
        out = t if out is None else out + t
    return out


def _sigmoid(x):
    return 1.0 / (1.0 + jnp.exp(-x))


def _log_sigmoid(x):
    return jnp.minimum(x, 0.0) - jnp.log1p(jnp.exp(-jnp.abs(x)))


def _rms_scale(x, g):
    var = jnp.mean(x * x, axis=-1, keepdims=True)
    return x * lax.rsqrt(var + EPS) * g


def _rmsnorm_kernel(x_ref, g_ref, o_ref):
    o_ref[...] = _rms_scale(x_ref[...], g_ref[...]).astype(o_ref.dtype)


def _rmsnorm_bf16(x, g):
    m, d = x.shape
    tm = _pick(m, (512, 256, 128))
    return pl.pallas_call(
        _rmsnorm_kernel,
        grid=(m // tm,),
        in_specs=[pl.BlockSpec((tm, d), lambda i: (i, 0)),
                  pl.BlockSpec((1, d), lambda i: (0, 0))],
        out_specs=pl.BlockSpec((tm, d), lambda i: (i, 0)),
        out_shape=jax.ShapeDtypeStruct((m, d), BF16),
        compiler_params=_params(1),
        name="rmsnorm_bf16",
    )(x, g.reshape(1, d))


def _causal_conv(u, cw, tpos, prev):
    k = cw.shape[0]
    y = u * cw[k - 1:k, :]
    for j in range(1, k):
        shifted = pltpu.roll(u, j, axis=0)
        fill = 0.0 if prev is None else prev[j - 1]
        y = y + jnp.where(tpos >= j, shifted, fill) * cw[k - 1 - j:k - j, :]
    return y


def _mixa_prompt_kernel(xn_ref, wh_ref, wb_ref, wc_ref, cw_ref, ya_ref, tail_ref):
    xn = xn_ref[...]
    u = _dot(xn, wc_ref[...]) * _dot(xn, wh_ref[...])
    tpos = lax.broadcasted_iota(jnp.int32, u.shape, 0)
    y = _causal_conv(u, cw_ref[...], tpos, None)
    ya_ref[...] = (_dot(xn, wb_ref[...]) * y).astype(ya_ref.dtype)
    nt = tail_ref.shape[0]
    tail_ref[...] = u[u.shape[0] - nt:, :]


def _mixa_sample_kernel(xn_ref, wh_ref, wb_ref, wc_ref, cw_ref, p1_ref, p2_ref, ya_ref, u_ref, *, seq):
    xn = xn_ref[...]
    u = _dot(xn, wc_ref[...]) * _dot(xn, wh_ref[...])
    tpos = lax.broadcasted_iota(jnp.int32, u.shape, 0) % seq
    y = _causal_conv(u, cw_ref[...], tpos, [p1_ref[...], p2_ref[...]])
    ya_ref[...] = (_dot(xn, wb_ref[...]) * y).astype(ya_ref.dtype)
    u_ref[...] = u


def _prev_rows(state, seq):
    b, km1, c = state.shape
    outs = []
    for j in range(1, km1 + 1):
        rows = [state[:, km1 - j + t, :] if t < j else jnp.zeros((b, c), state.dtype) for t in range(seq)]
        outs.append(jnp.stack(rows, axis=1).reshape(b * seq, c))
    return outs


def _new_state(state, u, seq):
    b, km1, c = state.shape
    ext = jnp.concatenate([state, u.reshape(b, seq, c)], axis=1)
    return ext[:, -km1:, :]


def _mixa_prompt(xn, wh, wb, wc, cw, nb, seq):
    m, d = xn.shape
    c = wh.width
    km1 = cw.shape[0] - 1
    tn = _pick(c, (256, 128))
    (wh, oh), (wb, ob), (wc, oc) = (_col_operand(w, tn) for w in (wh, wb, wc))
    wspec = lambda off: pl.BlockSpec((d, tn), lambda b, j: (0, j + off))
    return pl.pallas_call(
        _mixa_prompt_kernel,
        grid=(nb, c // tn),
        in_specs=[pl.BlockSpec((seq, d), lambda b, j: (b, 0)), wspec(oh), wspec(ob), wspec(oc),
                  pl.BlockSpec((km1 + 1, tn), lambda b, j: (0, j))],
        out_specs=[pl.BlockSpec((seq, tn), lambda b, j: (b, j)),
                   pl.BlockSpec((None, km1, tn), lambda b, j: (b, 0, j))],
        out_shape=[jax.ShapeDtypeStruct((m, c), BF16),
                   jax.ShapeDtypeStruct((nb, km1, c), F32)],
        compiler_params=_params(2),
        name="mixa_prompt",
    )(xn, wh, wb, wc, cw)


def _mixa_sample(xn, wh, wb, wc, cw, state, seq):
    m, d = xn.shape
    c = wh.width
    tn = _pick(c, (512, 256, 128))
    p1, p2 = _prev_rows(state, seq)
    (wh, oh), (wb, ob), (wc, oc) = (_col_operand(w, tn) for w in (wh, wb, wc))
    wspec = lambda off: pl.BlockSpec((d, tn), lambda j: (0, j + off))
    cspec = pl.BlockSpec((m, tn), lambda j: (0, j))
    ya, u = pl.pallas_call(
        functools.partial(_mixa_sample_kernel, seq=seq),
        grid=(c // tn,),
        in_specs=[pl.BlockSpec((m, d), lambda j: (0, 0)), wspec(oh), wspec(ob), wspec(oc),
                  pl.BlockSpec((cw.shape[0], tn), lambda j: (0, j)), cspec, cspec],
        out_specs=[cspec, cspec],
        out_shape=[jax.ShapeDtypeStruct((m, c), BF16), jax.ShapeDtypeStruct((m, c), F32)],
        compiler_params=_params(1),
        name="mixa_sample",
    )(xn, wh, wb, wc, cw, p1, p2)
    return ya, _new_state(state, u, seq)


def _qkv_kernel(xn_ref, wq_ref, wk_ref, wv_ref, q_ref, k_ref, v_ref, *, q_scale):
    xn = xn_ref[...]
    q_ref[...] = (_dot(xn, wq_ref[...]) * q_scale).astype(q_ref.dtype)
    k_ref[...] = _dot(xn, wk_ref[...])
    v_ref[...] = _dot(xn, wv_ref[...])


def _qkv(xn, wq, wk, wv, q_scale):
    m, d = xn.shape
    n = wq.width
    tm = _pick(m, (1024, 512, 256, 128))
    tn = _pick(n, (512, 256, 128))
    (wq, oq), (wk, ok), (wv, ov) = (_col_operand(w, tn) for w in (wq, wk, wv))
    wspec = lambda off: pl.BlockSpec((d, tn), lambda i, j: (0, j + off))
    ospec = pl.BlockSpec((tm, tn), lambda i, j: (i, j))
    return pl.pallas_call(
        functools.partial(_qkv_kernel, q_scale=q_scale),
        grid=(m // tm, n // tn),
        in_specs=[pl.BlockSpec((tm, d), lambda i, j: (i, 0)), wspec(oq), wspec(ok), wspec(ov)],
        out_specs=[ospec, ospec, ospec],
        out_shape=[jax.ShapeDtypeStruct((m, n), BF16), jax.ShapeDtypeStruct((m, n), F32),
                   jax.ShapeDtypeStruct((m, n), F32)],
        compiler_params=_params(2),
        name="qkv_proj",
    )(xn, wq, wk, wv)


def _gates_kernel(xn_ref, wa_ref, wb_ref, wf_ref, bf_ref, sa_ref, sb_ref, lf_ref):
    xn = xn_ref[...]
    sa_ref[...] = _sigmoid(_dot(xn, wa_ref[...])).astype(sa_ref.dtype)
    sb_ref[...] = _sigmoid(_dot(xn, wb_ref[...])).astype(sb_ref.dtype)

    @pl.when(pl.program_id(1) == 0)
    def _logforget():
        lf_ref[...] = _log_sigmoid(_dot(xn, wf_ref[...]) + bf_ref[...])


def _gates(xn, wa, wb, wf_pad, bf_pad):
    m, d = xn.shape
    n = wa.width
    tm = _pick(m, (1024, 512, 256, 128))
    tn = _pick(n, (512, 256, 128))
    (wa, oa), (wb, ob) = (_col_operand(w, tn) for w in (wa, wb))
    wspec = lambda off: pl.BlockSpec((d, tn), lambda i, j: (0, j + off))
    ospec = pl.BlockSpec((tm, tn), lambda i, j: (i, j))
    return pl.pallas_call(
        _gates_kernel,
        grid=(m // tm, n // tn),
        in_specs=[pl.BlockSpec((tm, d), lambda i, j: (i, 0)), wspec(oa), wspec(ob),
                  pl.BlockSpec((d, LANES), lambda i, j: (0, 0)),
                  pl.BlockSpec((1, LANES), lambda i, j: (0, 0))],
        out_specs=[ospec, ospec, pl.BlockSpec((tm, LANES), lambda i, j: (i, 0))],
        out_shape=[jax.ShapeDtypeStruct((m, n), BF16), jax.ShapeDtypeStruct((m, n), BF16),
                   jax.ShapeDtypeStruct((m, LANES), F32)],
        compiler_params=_params(2),
        name="gate_proj",
    )(xn, wa, wb, wf_pad, bf_pad)


def _cumsum_kernel(lf_ref, f_ref, *, blk):
    t = lf_ref.shape[0]
    r = lax.broadcasted_iota(jnp.int32, (blk, blk), 0)
    c = lax.broadcasted_iota(jnp.int32, (blk, blk), 1)
    lower = (c <= r).astype(BF16)
    carry = jnp.zeros((1, lf_ref.shape[1]), F32)
    for i in range(t // blk):
        y = _dot01(lower, lf_ref[i * blk:(i + 1) * blk, :], True) + carry
        f_ref[i * blk:(i + 1) * blk, :] = y
        carry = y[blk - 1:blk, :]


def _cumsum_rows(lf, nb, seq):
    blk = _pick(seq, (256, 128))
    return pl.pallas_call(
        functools.partial(_cumsum_kernel, blk=blk),
        grid=(nb,),
        in_specs=[pl.BlockSpec((seq, LANES), lambda b: (b, 0))],
        out_specs=pl.BlockSpec((seq, LANES), lambda b: (b, 0)),
        out_shape=jax.ShapeDtypeStruct(lf.shape, F32),
        compiler_params=_params(1),
        name="logforget_cumsum",
    )(lf)


def _fox_prompt_kernel(q_ref, k_ref, v_ref, fcol_ref, frow_ref, o_ref, *, hg, hd, nq):
    hgrp = pl.program_id(1)
    i = pl.program_id(2)
    tq = q_ref.shape[0]
    lane = lax.broadcasted_iota(jnp.int32, fcol_ref.shape, 1)
    cols_of = lambda e: slice(e * hd, (e + 1) * hd)

    def attend(c):
        past = c * tq
        rows = lax.broadcasted_iota(jnp.int32, (tq, tq), 0)
        cols = lax.broadcasted_iota(jnp.int32, (tq, tq), 1)
        for e in range(hg):
            q = q_ref[:, cols_of(e)]
            fq = jnp.sum(jnp.where(lane == hgrp * hg + e, fcol_ref[...], 0.0),
                         axis=-1, keepdims=True) * LOG2E
            kd = k_ref[past:past + tq, cols_of(e)].astype(BF16)
            td = _dot_nt(q, kd) - frow_ref[e:e + 1, past:past + tq] * LOG2E
            td = jnp.where(cols <= rows, td, -jnp.inf)
            mx = jnp.max(td, axis=-1, keepdims=True)
            if c:
                kp = k_ref[0:past, cols_of(e)].astype(BF16)
                tp = _dot_nt(q, kp) - frow_ref[e:e + 1, 0:past] * LOG2E
                mx = jnp.maximum(mx, jnp.max(tp, axis=-1, keepdims=True))
            shift = (mx + fq) - fq
            pd = jnp.exp2(td - shift)
            l = jnp.sum(pd, axis=-1, keepdims=True)
            acc = _dot(pd.astype(BF16), v_ref[past:past + tq, cols_of(e)].astype(BF16))
            if c:
                pp = jnp.exp2(tp - shift)
                l = l + jnp.sum(pp, axis=-1, keepdims=True)
                acc = acc + _dot(pp.astype(BF16), v_ref[0:past, cols_of(e)].astype(BF16))
            o_ref[:, cols_of(e)] = (acc / l).astype(o_ref.dtype)

    for c in range(nq):
        pl.when(i == c)(functools.partial(attend, c))


def _fox_prompt(q, k, v, fcum, nb, seq, nh, hd):
    m = q.shape[0]
    tq = _pick(seq, (512, 256, 128))
    nq = seq // tq
    hg = _pick(nh, (2, 1))
    ng = nh // hg
    frow = fcum[:, :nh].reshape(nb, seq, nh).transpose(0, 2, 1).reshape(nb * ng, hg, seq)
    return pl.pallas_call(
        functools.partial(_fox_prompt_kernel, hg=hg, hd=hd, nq=nq),
        grid=(nb, ng, nq),
        in_specs=[pl.BlockSpec((tq, hg * hd), lambda b, h, i: (b * nq + i, h)),
                  pl.BlockSpec((seq, hg * hd), lambda b, h, i: (b, h)),
                  pl.BlockSpec((seq, hg * hd), lambda b, h, i: (b, h)),
                  pl.BlockSpec((tq, LANES), lambda b, h, i: (b * nq + i, 0)),
                  pl.BlockSpec((None, hg, seq), lambda b, h, i: (b * ng + h, 0, 0))],
        out_specs=pl.BlockSpec((tq, hg * hd), lambda b, h, i: (b * nq + i, h)),
        out_shape=jax.ShapeDtypeStruct((m, nh * hd), BF16),
        compiler_params=_params(3),
        name="fox_prompt",
    )(q, k, v, fcum, frow)


def _suffix_kernel(lf_ref, o_ref, *, nh):
    w = lf_ref.shape[1]
    wo = o_ref.shape[1]
    shift = nh.bit_length() - 1
    r = lax.broadcasted_iota(jnp.int32, (w, wo), 0)
    c = lax.broadcasted_iota(jnp.int32, (w, wo), 1)
    same_head = jnp.bitwise_and(r, nh - 1) == jnp.bitwise_and(c, nh - 1)
    later_key = lax.shift_right_logical(r, shift) > lax.shift_right_logical(c, shift)
    sel = jnp.logical_and(same_head, jnp.logical_or(later_key, c >= w)).astype(BF16)
    o_ref[...] = _dot01(sel, lf_ref[...], False)


def _page_suffix(cache_lf_l):
    n_pool, page, nh = cache_lf_l.shape
    assert nh & (nh - 1) == 0 and LANES % nh == 0
    w = page * nh
    pp = _pick(n_pool, (512, 256, 128, 64, 32, 16, 8))
    out = pl.pallas_call(
        functools.partial(_suffix_kernel, nh=nh),
        grid=(n_pool // pp,),
        in_specs=[pl.BlockSpec((pp, w), lambda i: (i, 0))],
        out_specs=pl.BlockSpec((pp, w + LANES), lambda i: (i, 0)),
        out_shape=jax.ShapeDtypeStruct((n_pool, w + LANES), F32),
        compiler_params=_params(1),
        name="page_suffix",
    )(cache_lf_l.reshape(n_pool, w))
    return out.reshape(n_pool, 1, w + LANES)


def _fox_sample_kernel(pt_ref, q_ref, kn_ref, vn_ref, lfn_ref, *rest, n_pg, nh):
    del pt_ref
    k_refs = rest[:n_pg]
    v_refs = rest[n_pg:2 * n_pg]
    w_refs = rest[2 * n_pg:3 * n_pg]
    o_ref = rest[3 * n_pg]
    fn_ref, m_ref, l_ref, acc_ref, tail_ref, kcat_ref, vcat_ref = rest[3 * n_pg + 1:]
    g = pl.program_id(1)
    rows, hd = q_ref.shape
    t_new = rows // nh
    pw = k_refs[0].shape[0] * nh

    @pl.when(g == 0)
    def _init():
        eye = (lax.broadcasted_iota(jnp.int32, (nh, LANES), 0)
               == lax.broadcasted_iota(jnp.int32, (nh, LANES), 1))
        qf = q_ref[...].astype(F32)
        kn = kn_ref[...]
        vn = vn_ref[...]
        lfn = lfn_ref[...]
        fcol = []
        run = jnp.zeros((1, LANES), F32)
        for t in range(t_new):
            run = run + lfn[t:t + 1, :]
            fcol.append(jnp.sum(jnp.where(eye, jnp.broadcast_to(run, (nh, LANES)), 0.0),
                                axis=-1, keepdims=True) * LOG2E)
        for t in range(t_new):
            rs = slice(t * nh, (t + 1) * nh)
            s = [jnp.sum(qf[rs, :] * kn[j * nh:(j + 1) * nh, :], axis=-1, keepdims=True)
                 + (fcol[t] - fcol[j]) for j in range(t + 1)]
            m = s[0]
            for sj in s[1:]:
                m = jnp.maximum(m, sj)
            l = jnp.zeros((nh, 1), F32)
            acc = jnp.zeros((nh, hd), F32)
            for j, sj in enumerate(s):
                p = jnp.exp2(sj - m)
                l = l + p
                acc = acc + p * vn[j * nh:(j + 1) * nh, :]
            fn_ref[rs, :] = fcol[t]
            m_ref[rs, :] = m
            l_ref[rs, :] = l
            acc_ref[rs, :] = acc
        tail_ref[...] = jnp.zeros_like(tail_ref)

    reps = pw // LANES
    tail = tail_ref[...]
    bias = []
    for j in range(n_pg):
        kcat_ref[j * pw:(j + 1) * pw, :] = k_refs[j][...].reshape(pw, hd).astype(kcat_ref.dtype)
        vcat_ref[j * pw:(j + 1) * pw, :] = v_refs[j][...].reshape(pw, hd).astype(vcat_ref.dtype)
        w = w_refs[j][...]
        bias.append(w[:, :pw] + jnp.concatenate([tail] * reps, axis=1))
        tail = tail + w[:, pw:]
    tail_ref[...] = tail
    lane = lax.broadcasted_iota(jnp.int32, (rows, LANES), 1)
    row = lax.broadcasted_iota(jnp.int32, (rows, LANES), 0)
    own = jnp.bitwise_and(lane, nh - 1) == jnp.bitwise_and(row, nh - 1)
    mask = jnp.where(own, 0.0, -jnp.inf)
    bias = jnp.concatenate(bias, axis=1) * LOG2E
    t = _dot_nt(q_ref[...], kcat_ref[...])
    t = t + (bias + jnp.concatenate([mask] * (n_pg * reps), axis=1))
    fn = fn_ref[...]
    m_old = m_ref[...]
    m_new = jnp.maximum(m_old, jnp.max(t, axis=-1, keepdims=True) + fn)
    alpha = jnp.exp2(m_old - m_new)
    p = jnp.exp2(t - (m_new - fn))
    l_ref[...] = alpha * l_ref[...] + jnp.sum(p, axis=-1, keepdims=True)
    acc_ref[...] = alpha * acc_ref[...] + _dot(p.astype(BF16), vcat_ref[...])
    m_ref[...] = m_new

    @pl.when(g == pl.num_programs(1) - 1)
    def _finish():
        o_ref[...] = (acc_ref[...] / l_ref[...]).astype(o_ref.dtype)


def _fox_sample(q, k, v, lf, cache_k, cache_v, layer, wt, page_table, nbd, t_new, nh, hd):
    page = cache_k.shape[2]
    assert nh == SUBLANES, "a cached (head, head_dim) slab must be exactly one f32 vreg tile"
    n_pages = page_table.shape[1]
    n_pg = _pick(n_pages, (16, 8, 4, 2, 1))
    rows = t_new * nh
    pw = page * nh

    def page_index(b, g, pt, j):
        return pt[b, n_pages - 1 - (g * n_pg + j)]

    def cache_spec(j):
        return pl.BlockSpec((None, None, page, nh, hd),
                            lambda b, g, pt: (layer, page_index(b, g, pt, j), 0, 0, 0))

    def suffix_spec(j):
        return pl.BlockSpec((None, 1, pw + LANES), lambda b, g, pt: (page_index(b, g, pt, j), 0, 0))

    new_spec = pl.BlockSpec((None, rows, hd), lambda b, g, pt: (b, 0, 0))
    in_specs = [new_spec, new_spec, new_spec,
                pl.BlockSpec((None, t_new, LANES), lambda b, g, pt: (b, 0, 0))]
    in_specs += [cache_spec(j) for j in range(n_pg)]
    in_specs += [cache_spec(j) for j in range(n_pg)]
    in_specs += [suffix_spec(j) for j in range(n_pg)]
    grid_spec = pltpu.PrefetchScalarGridSpec(
        num_scalar_prefetch=1,
        grid=(nbd, n_pages // n_pg),
        in_specs=in_specs,
        out_specs=new_spec,
        scratch_shapes=[pltpu.VMEM((rows, 1), F32),
                        pltpu.VMEM((rows, 1), F32),
                        pltpu.VMEM((rows, 1), F32),
                        pltpu.VMEM((rows, hd), F32),
                        pltpu.VMEM((1, LANES), F32),
                        pltpu.VMEM((n_pg * pw, hd), BF16),
                        pltpu.VMEM((n_pg * pw, hd), BF16)])
    to_rows = lambda a: a.reshape(nbd, rows, hd)
    out = pl.pallas_call(
        functools.partial(_fox_sample_kernel, n_pg=n_pg, nh=nh),
        grid_spec=grid_spec,
        out_shape=jax.ShapeDtypeStruct((nbd, rows, hd), BF16),
        compiler_params=_params(2),
        name="fox_sample",
    )(page_table, to_rows(q), to_rows(k), to_rows(v), lf.reshape(nbd, t_new, LANES),
      *([cache_k] * n_pg), *([cache_v] * n_pg), *([wt] * n_pg))
    return out.reshape(nbd * t_new, nh * hd)


def _merge_kernel(ya_ref, o_ref, woa_ref, wob_ref, sa_ref, sb_ref, m_ref):
    a = _dot(ya_ref[...], woa_ref[...])
    b = _dot(o_ref[...], wob_ref[...])
    m_ref[...] = (sa_ref[...].astype(F32) * a + sb_ref[...].astype(F32) * b).astype(m_ref.dtype)


def _merge(ya, o, woa, wob, sa, sb):
    m, ca = ya.shape
    cb = o.shape[1]
    n = woa.shape[1]
    tm = _pick(m, (2048, 1024, 512, 256, 128))
    tn = _pick(n, (512, 256, 128))
    gspec = pl.BlockSpec((tm, tn), lambda i, j: (i, j))
    return pl.pallas_call(
        _merge_kernel,
        grid=(m // tm, n // tn),
        in_specs=[pl.BlockSpec((tm, ca), lambda i, j: (i, 0)),
                  pl.BlockSpec((tm, cb), lambda i, j: (i, 0)),
                  pl.BlockSpec((ca, tn), lambda i, j: (0, j)),
                  pl.BlockSpec((cb, tn), lambda i, j: (0, j)), gspec, gspec],
        out_specs=gspec,
        out_shape=jax.ShapeDtypeStruct((m, n), BF16),
        compiler_params=_params(2),
        name="mixer_merge",
    )(ya, o, woa, wob, sa, sb)


def _oproj_kernel(mix_ref, wo_ref, x_ref, g_ref, x1_ref, xn_ref):
    x1 = x_ref[...] + _dot(mix_ref[...], wo_ref[...])
    x1_ref[...] = x1
    xn_ref[...] = _rms_scale(x1, g_ref[...]).astype(xn_ref.dtype)


def _oproj(mix, wo, x, g):
    m, d = x.shape
    tm = _pick(m, (512, 256, 128))
    rspec = pl.BlockSpec((tm, d), lambda i: (i, 0))
    return pl.pallas_call(
        _oproj_kernel,
        grid=(m // tm,),
        in_specs=[rspec, pl.BlockSpec((d, d), lambda i: (0, 0)), rspec,
                  pl.BlockSpec((1, d), lambda i: (0, 0))],
        out_specs=[rspec, rspec],
        out_shape=[jax.ShapeDtypeStruct((m, d), F32), jax.ShapeDtypeStruct((m, d), BF16)],
        compiler_params=_params(1),
        name="oproj_residual_norm",
    )(mix, wo, x, g.reshape(1, d))


def _ffn_up_prompt_kernel(xn_ref, wu_ref, wg_ref, cw_ref, h_ref, tail_ref):
    xn = xn_ref[...]
    gpre = _dot(xn, wg_ref[...])
    tpos = lax.broadcasted_iota(jnp.int32, gpre.shape, 0)
    gc = _causal_conv(gpre, cw_ref[...], tpos, None)
    h_ref[...] = (gc * _sigmoid(gc) * _dot(xn, wu_ref[...])).astype(h_ref.dtype)
    nt = tail_ref.shape[0]
    tail_ref[...] = gpre[gpre.shape[0] - nt:, :]


def _ffn_up_sample_kernel(xn_ref, wu_ref, wg_ref, cw_ref, p1_ref, p2_ref, h_ref, g_ref, *, seq):
    xn = xn_ref[...]
    gpre = _dot(xn, wg_ref[...])
    tpos = lax.broadcasted_iota(jnp.int32, gpre.shape, 0) % seq
    gc = _causal_conv(gpre, cw_ref[...], tpos, [p1_ref[...], p2_ref[...]])
    h_ref[...] = (gc * _sigmoid(gc) * _dot(xn, wu_ref[...])).astype(h_ref.dtype)
    g_ref[...] = gpre


def _ffn_up_prompt(xn, wu, wg, cw, nb, seq):
    m, d = xn.shape
    f = wu.shape[1]
    km1 = cw.shape[0] - 1
    tn = _pick(f, (512, 256, 128))
    wspec = pl.BlockSpec((d, tn), lambda b, j: (0, j))
    return pl.pallas_call(
        _ffn_up_prompt_kernel,
        grid=(nb, f // tn),
        in_specs=[pl.BlockSpec((seq, d), lambda b, j: (b, 0)), wspec, wspec,
                  pl.BlockSpec((km1 + 1, tn), lambda b, j: (0, j))],
        out_specs=[pl.BlockSpec((seq, tn), lambda b, j: (b, j)),
                   pl.BlockSpec((None, km1, tn), lambda b, j: (b, 0, j))],
        out_shape=[jax.ShapeDtypeStruct((m, f), BF16), jax.ShapeDtypeStruct((nb, km1, f), F32)],
        compiler_params=_params(2),
        name="ffn_up_prompt",
    )(xn, wu, wg, cw)


def _ffn_up_sample(xn, wu, wg, cw, state, seq):
    m, d = xn.shape
    f = wu.shape[1]
    tn = _pick(f, (512, 256, 128))
    p1, p2 = _prev_rows(state, seq)
    wspec = pl.BlockSpec((d, tn), lambda j: (0, j))
    cspec = pl.BlockSpec((m, tn), lambda j: (0, j))
    h, gpre = pl.pallas_call(
        functools.partial(_ffn_up_sample_kernel, seq=seq),
        grid=(f // tn,),
        in_specs=[pl.BlockSpec((m, d), lambda j: (0, 0)), wspec, wspec,
                  pl.BlockSpec((cw.shape[0], tn), lambda j: (0, j)), cspec, cspec],
        out_specs=[cspec, cspec],
        out_shape=[jax.ShapeDtypeStruct((m, f), BF16), jax.ShapeDtypeStruct((m, f), F32)],
        compiler_params=_params(1),
        name="ffn_up_sample",
    )(xn, wu, wg, cw, p1, p2)
    return h, _new_state(state, gpre, seq)


def _ffn_down_kernel(h_ref, wd_ref, x1_ref, g_ref, y_ref, *, normalize, nk):
    kk = pl.program_id(1)

    @pl.when(kk == 0)
    def _first():
        y_ref[...] = x1_ref[...]

    y_ref[...] += _dot(h_ref[...], wd_ref[...])

    if normalize:
        @pl.when(kk == nk - 1)
        def _last():
            y_ref[...] = _rms_scale(y_ref[...], g_ref[...])


def _ffn_down(h, wd, x1, g, normalize):
    m, f = h.shape
    d = x1.shape[1]
    tm = _pick(m, (512, 256, 128))
    tk = _pick(f, (2816, 2048, 1024, 512, 256, 128))
    return pl.pallas_call(
        functools.partial(_ffn_down_kernel, normalize=normalize, nk=f // tk),
        grid=(m // tm, f // tk),
        in_specs=[pl.BlockSpec((tm, tk), lambda i, kk: (i, kk)),
                  pl.BlockSpec((tk, d), lambda i, kk: (kk, 0)),
                  pl.BlockSpec((tm, d), lambda i, kk: (i, 0), pipeline_mode=pl.Buffered(1)),
                  pl.BlockSpec((1, d), lambda i, kk: (0, 0))],
        out_specs=pl.BlockSpec((tm, d), lambda i, kk: (i, 0)),
        out_shape=jax.ShapeDtypeStruct((m, d), F32),
        compiler_params=_params(2),
        name="ffn_down_residual_norm",
    )(h, wd, x1, g.reshape(1, d))


def _in_proj_weights(w_in_l, b_f_l, d_conv, d_attn, nh, d_model):
    w = w_in_l.astype(BF16)
    sizes = (d_conv, d_conv, d_conv, d_attn, d_attn, d_attn, nh, d_model, d_model)
    cols, start = [], 0
    for s in sizes:
        cols.append(_Cols(w, start, s))
        start += s
    wf = cols[6]
    wf_pad = jnp.pad(w[:, wf.start:wf.start + nh], ((0, 0), (0, LANES - nh)))
    bf_pad = jnp.pad(b_f_l.astype(F32), (0, LANES - nh)).reshape(1, LANES)
    return cols[:6], cols[7:], wf_pad, bf_pad


def kernel(x_prompt, x_sample, cache_k, cache_v, cache_lf, state_conv_a, state_conv_ffn, page_table,
           norm_mix_g, w_in, b_f, conv_a_w, w_out_a, w_out_b, w_o, norm_ffn_g, w_up, w_gate,
           conv_ffn_w, w_down, norm_final_g):
    bp, tp, d_model = x_prompt.shape
    bd, td, _ = x_sample.shape
    depth = w_in.shape[0]
    nh, hd = cache_k.shape[3], cache_k.shape[4]
    d_attn = nh * hd
    d_conv = conv_a_w.shape[2]
    assert conv_a_w.shape[1] == 3 and conv_ffn_w.shape[1] == 3
    assert nh <= SUBLANES and hd % LANES == 0
    q_scale = hd ** -0.5 * LOG2E

    hp = x_prompt.reshape(bp * tp, d_model)
    hs = x_sample.reshape(bd * td, d_model)
    outs = [[] for _ in range(10)]
    for l in range(depth):
        (wh, wb, wc, wq, wk, wv), (wga, wgb), wf_pad, bf_pad = _in_proj_weights(
            w_in[l], b_f[l], d_conv, d_attn, nh, d_model)
        woa, wob, wo = w_out_a[l].astype(BF16), w_out_b[l].astype(BF16), w_o[l].astype(BF16)
        wu, wg, wd = w_up[l].astype(BF16), w_gate[l].astype(BF16), w_down[l].astype(BF16)
        last = l == depth - 1

        xn = _rmsnorm_bf16(hp, norm_mix_g[l])
        ya, cap = _mixa_prompt(xn, wh, wb, wc, conv_a_w[l], bp, tp)
        q, kp, vp = _qkv(xn, wq, wk, wv, q_scale)
        sa, sb, lfp = _gates(xn, wga, wgb, wf_pad, bf_pad)
        fcum = _cumsum_rows(lfp, bp, tp)
        o = _fox_prompt(q, kp, vp, fcum, bp, tp, nh, hd)
        mix = _merge(ya, o, woa, wob, sa, sb)
        x1, xn2 = _oproj(mix, wo, hp, norm_ffn_g[l])
        hact, cfp = _ffn_up_prompt(xn2, wu, wg, conv_ffn_w[l], bp, tp)
        hp = _ffn_down(hact, wd, x1, norm_final_g, last)

        xn = _rmsnorm_bf16(hs, norm_mix_g[l])
        ya, cas = _mixa_sample(xn, wh, wb, wc, conv_a_w[l], state_conv_a[l], td)
        q, ks, vs = _qkv(xn, wq, wk, wv, q_scale)
        sa, sb, lfs = _gates(xn, wga, wgb, wf_pad, bf_pad)
        wt = _page_suffix(cache_lf[l])
        o = _fox_sample(q, ks, vs, lfs, cache_k, cache_v, l, wt, page_table, bd, td, nh, hd)
        mix = _merge(ya, o, woa, wob, sa, sb)
        x1s, xn2 = _oproj(mix, wo, hs, norm_ffn_g[l])
        hact, cfs = _ffn_up_sample(xn2, wu, wg, conv_ffn_w[l], state_conv_ffn[l], td)
        hs = _ffn_down(hact, wd, x1s, norm_final_g, last)

        for lst, val in zip(outs, (kp.reshape(bp, tp, nh, hd), vp.reshape(bp, tp, nh, hd),
                                   lfp[:, :nh].reshape(bp, tp, nh), ks.reshape(bd, td, nh, hd),
                                   vs.reshape(bd, td, nh, hd), lfs[:, :nh].reshape(bd, td, nh),
                                   cap, cas, cfp, cfs)):
            lst.append(val)
    return (hp.reshape(bp, tp, d_model), hs.reshape(bd, td, d_model), *[jnp.stack(o_) for o_ in outs])
```

```python
import functools
from typing import NamedTuple

import jax
import jax.numpy as jnp
from jax import lax
from jax.experimental import pallas as pl
from jax.experimental.pallas import tpu as pltpu

F32 = jnp.float32
BF16 = jnp.bfloat16
EPS = 1e-6
LANES = 128
SUBLANES = 8
VMEM_LIMIT = 56 * 1024 * 1024
LOG2E = 1.4426950408889634


class _Cols(NamedTuple):
    array: jax.Array
    start: int
    width: int


def _col_operand(cols, tn):
    if cols.start % tn == 0:
        return cols.array, cols.start // tn
    return cols.array[:, cols.start:cols.start + cols.width], 0


def _pick(n, cands):
    for c in cands:
        if n % c == 0:
            return c
    return n


def _params(n_axes):
    return pltpu.CompilerParams(dimension_semantics=("arbitrary",) * n_axes,
                                vmem_limit_bytes=VMEM_LIMIT)


def _dot(a, b):
    return jnp.dot(a, b, preferred_element_type=F32)


def _dot_nt(a, b):
    return lax.dot_general(a, b, (((1,), (1,)), ((), ())), preferred_element_type=F32)


def _split3(x):
    hi = x.astype(BF16)
    r1 = x - hi.astype(F32)
    mid = r1.astype(BF16)
    lo = (r1 - mid.astype(F32)).astype(BF16)
    return hi, mid, lo


def _dot01(ones_mat, x, ones_on_left):
    out = None
    for part in _split3(x):
        t = _dot(ones_mat, part) if ones_on_left else _dot(part, ones_mat)
        out = t if out is None else out + t
    return out


def _sigmoid(x):
    return 1.0 / (1.0 + jnp.exp(-x))


def _log_sigmoid(x):
    return jnp.minimum(x, 0.0) - jnp.log1p(jnp.exp(-jnp.abs(x)))


def _rms_scale(x, g):
    var = jnp.mean(x * x, axis=-1, keepdims=True)
    return x * lax.rsqrt(var + EPS) * g


def _rmsnorm_kernel(x_ref, g_ref, o_ref):
    o_ref[...] = _rms_scale(x_ref[...], g_ref[...]).astype(o_ref.dtype)


def _rmsnorm_bf16(x, g):
    m, d = x.shape
    tm = _pick(m, (512, 256, 128))
    return pl.pallas_call(
        _rmsnorm_kernel,
        grid=(m // tm,),
        in_specs=[pl.BlockSpec((tm, d), lambda i: (i, 0)),
                  pl.BlockSpec((1, d), lambda i: (0, 0))],
        out_specs=pl.BlockSpec((tm, d), lambda i: (i, 0)),
        out_shape=jax.ShapeDtypeStruct((m, d), BF16),
        compiler_params=_params(1),
        name="rmsnorm_bf16",
    )(x, g.reshape(1, d))


def _causal_conv(u, cw, tpos, prev):
    k = cw.shape[0]
    y = u * cw[k - 1:k, :]
    for j in range(1, k):
        shifted = pltpu.roll(u, j, axis=0)
        fill = 0.0 if prev is None else prev[j - 1]
        y = y + jnp.where(tpos >= j, shifted, fill) * cw[k - 1 - j:k - j, :]
    return y


def _mixa_prompt_kernel(xn_ref, wh_ref, wb_ref, wc_ref, cw_ref, ya_ref, tail_ref):
    xn = xn_ref[...]
    u = _dot(xn, wc_ref[...]) * _dot(xn, wh_ref[...])
    tpos = lax.broadcasted_iota(jnp.int32, u.shape, 0)
    y = _causal_conv(u, cw_ref[...], tpos, None)
    ya_ref[...] = (_dot(xn, wb_ref[...]) * y).astype(ya_ref.dtype)
    nt = tail_ref.shape[0]
    tail_ref[...] = u[u.shape[0] - nt:, :]


def _mixa_sample_kernel(xn_ref, wh_ref, wb_ref, wc_ref, cw_ref, p1_ref, p2_ref, ya_ref, u_ref, *, seq):
    xn = xn_ref[...]
    u = _dot(xn, wc_ref[...]) * _dot(xn, wh_ref[...])
    tpos = lax.broadcasted_iota(jnp.int32, u.shape, 0) % seq
    y = _causal_conv(u, cw_ref[...], tpos, [p1_ref[...], p2_ref[...]])
    ya_ref[...] = (_dot(xn, wb_ref[...]) * y).astype(ya_ref.dtype)
    u_ref[...] = u


def _prev_rows(state, seq):
    b, km1, c = state.shape
    outs = []
    for j in range(1, km1 + 1):
        rows = [state[:, km1 - j + t, :] if t < j else jnp.zeros((b, c), state.dtype) for t in range(seq)]
        outs.append(jnp.stack(rows, axis=1).reshape(b * seq, c))
    return outs


def _new_state(state, u, seq):
    b, km1, c = state.shape
    ext = jnp.concatenate([state, u.reshape(b, seq, c)], axis=1)
    return ext[:, -km1:, :]


def _mixa_prompt(xn, wh, wb, wc, cw, nb, seq):
    m, d = xn.shape
    c = wh.width
    km1 = cw.shape[0] - 1
    tn = _pick(c, (256, 128))
    (wh, oh), (wb, ob), (wc, oc) = (_col_operand(w, tn) for w in (wh, wb, wc))
    wspec = lambda off: pl.BlockSpec((d, tn), lambda b, j: (0, j + off))
    return pl.pallas_call(
        _mixa_prompt_kernel,
        grid=(nb, c // tn),
        in_specs=[pl.BlockSpec((seq, d), lambda b, j: (b, 0)), wspec(oh), wspec(ob), wspec(oc),
                  pl.BlockSpec((km1 + 1, tn), lambda b, j: (0, j))],
        out_specs=[pl.BlockSpec((seq, tn), lambda b, j: (b, j)),
                   pl.BlockSpec((None, km1, tn), lambda b, j: (b, 0, j))],
        out_shape=[jax.ShapeDtypeStruct((m, c), BF16),
                   jax.ShapeDtypeStruct((nb, km1, c), F32)],
        compiler_params=_params(2),
        name="mixa_prompt",
    )(xn, wh, wb, wc, cw)


def _mixa_sample(xn, wh, wb, wc, cw, state, seq):
    m, d = xn.shape
    c = wh.width
    tn = _pick(c, (512, 256, 128))
    p1, p2 = _prev_rows(state, seq)
    (wh, oh), (wb, ob), (wc, oc) = (_col_operand(w, tn) for w in (wh, wb, wc))
    wspec = lambda off: pl.BlockSpec((d, tn), lambda j: (0, j + off))
    cspec = pl.BlockSpec((m, tn), lambda j: (0, j))
    ya, u = pl.pallas_call(
        functools.partial(_mixa_sample_kernel, seq=seq),
        grid=(c // tn,),
        in_specs=[pl.BlockSpec((m, d), lambda j: (0, 0)), wspec(oh), wspec(ob), wspec(oc),
                  pl.BlockSpec((cw.shape[0], tn), lambda j: (0, j)), cspec, cspec],
        out_specs=[cspec, cspec],
        out_shape=[jax.ShapeDtypeStruct((m, c), BF16), jax.ShapeDtypeStruct((m, c), F32)],
        compiler_params=_params(1),
        name="mixa_sample",
    )(xn, wh, wb, wc, cw, p1, p2)
    return ya, _new_state(state, u, seq)


def _qkv_kernel(xn_ref, wq_ref, wk_ref, wv_ref, q_ref, k_ref, v_ref, *, q_scale):
    xn = xn_ref[...]
    q_ref[...] = (_dot(xn, wq_ref[...]) * q_scale).astype(q_ref.dtype)
    k_ref[...] = _dot(xn, wk_ref[...])
    v_ref[...] = _dot(xn, wv_ref[...])


def _qkv(xn, wq, wk, wv, q_scale):
    m, d = xn.shape
    n = wq.width
    tm = _pick(m, (1024, 512, 256, 128))
    tn = _pick(n, (512, 256, 128))
    (wq, oq), (wk, ok), (wv, ov) = (_col_operand(w, tn) for w in (wq, wk, wv))
    wspec = lambda off: pl.BlockSpec((d, tn), lambda i, j: (0, j + off))
    ospec = pl.BlockSpec((tm, tn), lambda i, j: (i, j))
    return pl.pallas_call(
        functools.partial(_qkv_kernel, q_scale=q_scale),
        grid=(m // tm, n // tn),
        in_specs=[pl.BlockSpec((tm, d), lambda i, j: (i, 0)), wspec(oq), wspec(ok), wspec(ov)],
        out_specs=[ospec, ospec, ospec],
        out_shape=[jax.ShapeDtypeStruct((m, n), BF16), jax.ShapeDtypeStruct((m, n), F32),
                   jax.ShapeDtypeStruct((m, n), F32)],
        compiler_params=_params(2),
        name="qkv_proj",
    )(xn, wq, wk, wv)


def _gates_kernel(xn_ref, wa_ref, wb_ref, wf_ref, bf_ref, sa_ref, sb_ref, lf_ref):
    xn = xn_ref[...]
    sa_ref[...] = _sigmoid(_dot(xn, wa_ref[...])).astype(sa_ref.dtype)
    sb_ref[...] = _sigmoid(_dot(xn, wb_ref[...])).astype(sb_ref.dtype)

    @pl.when(pl.program_id(1) == 0)
    def _logforget():
        lf_ref[...] = _log_sigmoid(_dot(xn, wf_ref[...]) + bf_ref[...])


def _gates(xn, wa, wb, wf_pad, bf_pad):
    m, d = xn.shape
    n = wa.width
    tm = _pick(m, (1024, 512, 256, 128))
    tn = _pick(n, (512, 256, 128))
    (wa, oa), (wb, ob) = (_col_operand(w, tn) for w in (wa, wb))
    wspec = lambda off: pl.BlockSpec((d, tn), lambda i, j: (0, j + off))
    ospec = pl.BlockSpec((tm, tn), lambda i, j: (i, j))
    return pl.pallas_call(
        _gates_kernel,
        grid=(m // tm, n // tn),
        in_specs=[pl.BlockSpec((tm, d), lambda i, j: (i, 0)), wspec(oa), wspec(ob),
                  pl.BlockSpec((d, LANES), lambda i, j: (0, 0)),
                  pl.BlockSpec((1, LANES), lambda i, j: (0, 0))],
        out_specs=[ospec, ospec, pl.BlockSpec((tm, LANES), lambda i, j: (i, 0))],
        out_shape=[jax.ShapeDtypeStruct((m, n), BF16), jax.ShapeDtypeStruct((m, n), BF16),
                   jax.ShapeDtypeStruct((m, LANES), F32)],
        compiler_params=_params(2),
        name="gate_proj",
    )(xn, wa, wb, wf_pad, bf_pad)


def _cumsum_kernel(lf_ref, f_ref, *, blk):
    t = lf_ref.shape[0]
    r = lax.broadcasted_iota(jnp.int32, (blk, blk), 0)
    c = lax.broadcasted_iota(jnp.int32, (blk, blk), 1)
    lower = (c <= r).astype(BF16)
    carry = jnp.zeros((1, lf_ref.shape[1]), F32)
    for i in range(t // blk):
        y = _dot01(lower, lf_ref[i * blk:(i + 1) * blk, :], True) + carry
        f_ref[i * blk:(i + 1) * blk, :] = y
        carry = y[blk - 1:blk, :]


def _cumsum_rows(lf, nb, seq):
    blk = _pick(seq, (256, 128))
    return pl.pallas_call(
        functools.partial(_cumsum_kernel, blk=blk),
        grid=(nb,),
        in_specs=[pl.BlockSpec((seq, LANES), lambda b: (b, 0))],
        out_specs=pl.BlockSpec((seq, LANES), lambda b: (b, 0)),
        out_shape=jax.ShapeDtypeStruct(lf.shape, F32),
        compiler_params=_params(1),
        name="logforget_cumsum",
    )(lf)


def _fox_prompt_kernel(q_ref, k_ref, v_ref, fcol_ref, frow_ref, o_ref, *, hg, hd, nq):
    hgrp = pl.program_id(1)
    i = pl.program_id(2)
    tq = q_ref.shape[0]
    lane = lax.broadcasted_iota(jnp.int32, fcol_ref.shape, 1)
    cols_of = lambda e: slice(e * hd, (e + 1) * hd)

    def attend(c):
        past = c * tq
        rows = lax.broadcasted_iota(jnp.int32, (tq, tq), 0)
        cols = lax.broadcasted_iota(jnp.int32, (tq, tq), 1)
        for e in range(hg):
            q = q_ref[:, cols_of(e)]
            fq = jnp.sum(jnp.where(lane == hgrp * hg + e, fcol_ref[...], 0.0),
                         axis=-1, keepdims=True) * LOG2E
            kd = k_ref[past:past + tq, cols_of(e)].astype(BF16)
            td = _dot_nt(q, kd) - frow_ref[e:e + 1, past:past + tq] * LOG2E
            td = jnp.where(cols <= rows, td, -jnp.inf)
            mx = jnp.max(td, axis=-1, keepdims=True)
            if c:
                kp = k_ref[0:past, cols_of(e)].astype(BF16)
                tp = _dot_nt(q, kp) - frow_ref[e:e + 1, 0:past] * LOG2E
                mx = jnp.maximum(mx, jnp.max(tp, axis=-1, keepdims=True))
            shift = (mx + fq) - fq
            pd = jnp.exp2(td - shift)
            l = jnp.sum(pd, axis=-1, keepdims=True)
            acc = _dot(pd.astype(BF16), v_ref[past:past + tq, cols_of(e)].astype(BF16))
            if c:
                pp = jnp.exp2(tp - shift)
                l = l + jnp.sum(pp, axis=-1, keepdims=True)
                acc = acc + _dot(pp.astype(BF16), v_ref[0:past, cols_of(e)].astype(BF16))
            o_ref[:, cols_of(e)] = (acc / l).astype(o_ref.dtype)

    for c in range(nq):
        pl.when(i == c)(functools.partial(attend, c))


def _fox_prompt(q, k, v, fcum, nb, seq, nh, hd):
    m = q.shape[0]
    tq = _pick(seq, (512, 256, 128))
    nq = seq // tq
    hg = _pick(nh, (4, 2, 1))
    ng = nh // hg
    frow = fcum[:, :nh].reshape(nb, seq, nh).transpose(0, 2, 1).reshape(nb * ng, hg, seq)
    return pl.pallas_call(
        functools.partial(_fox_prompt_kernel, hg=hg, hd=hd, nq=nq),
        grid=(nb, ng, nq),
        in_specs=[pl.BlockSpec((tq, hg * hd), lambda b, h, i: (b * nq + i, h)),
                  pl.BlockSpec((seq, hg * hd), lambda b, h, i: (b, h)),
                  pl.BlockSpec((seq, hg * hd), lambda b, h, i: (b, h)),
                  pl.BlockSpec((tq, LANES), lambda b, h, i: (b * nq + i, 0)),
                  pl.BlockSpec((None, hg, seq), lambda b, h, i: (b * ng + h, 0, 0))],
        out_specs=pl.BlockSpec((tq, hg * hd), lambda b, h, i: (b * nq + i, h)),
        out_shape=jax.ShapeDtypeStruct((m, nh * hd), BF16),
        compiler_params=_params(3),
        name="fox_prompt",
    )(q, k, v, fcum, frow)


def _suffix_kernel(lf_ref, o_ref, *, nh):
    w = lf_ref.shape[1]
    wo = o_ref.shape[1]
    shift = nh.bit_length() - 1
    r = lax.broadcasted_iota(jnp.int32, (w, wo), 0)
    c = lax.broadcasted_iota(jnp.int32, (w, wo), 1)
    same_head = jnp.bitwise_and(r, nh - 1) == jnp.bitwise_and(c, nh - 1)
    later_key = lax.shift_right_logical(r, shift) > lax.shift_right_logical(c, shift)
    sel = jnp.logical_and(same_head, jnp.logical_or(later_key, c >= w)).astype(BF16)
    o_ref[...] = _dot01(sel, lf_ref[...], False)


def _page_suffix(cache_lf_l):
    n_pool, page, nh = cache_lf_l.shape
    assert nh & (nh - 1) == 0 and LANES % nh == 0
    w = page * nh
    pp = _pick(n_pool, (512, 256, 128, 64, 32, 16, 8))
    out = pl.pallas_call(
        functools.partial(_suffix_kernel, nh=nh),
        grid=(n_pool // pp,),
        in_specs=[pl.BlockSpec((pp, w), lambda i: (i, 0))],
        out_specs=pl.BlockSpec((pp, w + LANES), lambda i: (i, 0)),
        out_shape=jax.ShapeDtypeStruct((n_pool, w + LANES), F32),
        compiler_params=_params(1),
        name="page_suffix",
    )(cache_lf_l.reshape(n_pool, w))
    return out.reshape(n_pool, 1, w + LANES)


def _fox_sample_kernel(pt_ref, q_ref, kn_ref, vn_ref, lfn_ref, *rest, n_pg, nh):
    del pt_ref
    k_refs = rest[:n_pg]
    v_refs = rest[n_pg:2 * n_pg]
    w_refs = rest[2 * n_pg:3 * n_pg]
    o_ref = rest[3 * n_pg]
    fn_ref, m_ref, l_ref, acc_ref, tail_ref, kcat_ref, vcat_ref = rest[3 * n_pg + 1:]
    g = pl.program_id(1)
    rows, hd = q_ref.shape
    t_new = rows // nh
    pw = k_refs[0].shape[0] * nh

    @pl.when(g == 0)
    def _init():
        eye = (lax.broadcasted_iota(jnp.int32, (nh, LANES), 0)
               == lax.broadcasted_iota(jnp.int32, (nh, LANES), 1))
        qf = q_ref[...].astype(F32)
        kn = kn_ref[...]
        vn = vn_ref[...]
        lfn = lfn_ref[...]
        fcol = []
        run = jnp.zeros((1, LANES), F32)
        for t in range(t_new):
            run = run + lfn[t:t + 1, :]
            fcol.append(jnp.sum(jnp.where(eye, jnp.broadcast_to(run, (nh, LANES)), 0.0),
                                axis=-1, keepdims=True) * LOG2E)
        for t in range(t_new):
            rs = slice(t * nh, (t + 1) * nh)
            s = [jnp.sum(qf[rs, :] * kn[j * nh:(j + 1) * nh, :], axis=-1, keepdims=True)
                 + (fcol[t] - fcol[j]) for j in range(t + 1)]
            m = s[0]
            for sj in s[1:]:
                m = jnp.maximum(m, sj)
            l = jnp.zeros((nh, 1), F32)
            acc = jnp.zeros((nh, hd), F32)
            for j, sj in enumerate(s):
                p = jnp.exp2(sj - m)
                l = l + p
                acc = acc + p * vn[j * nh:(j + 1) * nh, :]
            fn_ref[rs, :] = fcol[t]
            m_ref[rs, :] = m
            l_ref[rs, :] = l
            acc_ref[rs, :] = acc
        tail_ref[...] = jnp.zeros_like(tail_ref)

    reps = pw // LANES
    tail = tail_ref[...]
    bias = []
    for j in range(n_pg):
        kcat_ref[j * pw:(j + 1) * pw, :] = k_refs[j][...].reshape(pw, hd).astype(kcat_ref.dtype)
        vcat_ref[j * pw:(j + 1) * pw, :] = v_refs[j][...].reshape(pw, hd).astype(vcat_ref.dtype)
        w = w_refs[j][...]
        bias.append(w[:, :pw] + jnp.concatenate([tail] * reps, axis=1))
        tail = tail + w[:, pw:]
    tail_ref[...] = tail
    lane = lax.broadcasted_iota(jnp.int32, (rows, LANES), 1)
    row = lax.broadcasted_iota(jnp.int32, (rows, LANES), 0)
    own = jnp.bitwise_and(lane, nh - 1) == jnp.bitwise_and(row, nh - 1)
    mask = jnp.where(own, 0.0, -jnp.inf)
    bias = jnp.concatenate(bias, axis=1) * LOG2E
    t = _dot_nt(q_ref[...], kcat_ref[...])
    t = t + (bias + jnp.concatenate([mask] * (n_pg * reps), axis=1))
    fn = fn_ref[...]
    m_old = m_ref[...]
    m_new = jnp.maximum(m_old, jnp.max(t, axis=-1, keepdims=True) + fn)
    alpha = jnp.exp2(m_old - m_new)
    p = jnp.exp2(t - (m_new - fn))
    l_ref[...] = alpha * l_ref[...] + jnp.sum(p, axis=-1, keepdims=True)
    acc_ref[...] = alpha * acc_ref[...] + _dot(p.astype(BF16), vcat_ref[...])
    m_ref[...] = m_new

    @pl.when(g == pl.num_programs(1) - 1)
    def _finish():
        o_ref[...] = (acc_ref[...] / l_ref[...]).astype(o_ref.dtype)


def _fox_sample(q, k, v, lf, cache_k, cache_v, layer, wt, page_table, nbd, t_new, nh, hd):
    page = cache_k.shape[2]
    assert nh == SUBLANES, "a cached (head, head_dim) slab must be exactly one f32 vreg tile"
    n_pages = page_table.shape[1]
    n_pg = _pick(n_pages, (16, 8, 4, 2, 1))
    rows = t_new * nh
    pw = page * nh

    def page_index(b, g, pt, j):
        return pt[b, n_pages - 1 - (g * n_pg + j)]

    def cache_spec(j):
        return pl.BlockSpec((None, None, page, nh, hd),
                            lambda b, g, pt: (layer, page_index(b, g, pt, j), 0, 0, 0))

    def suffix_spec(j):
        return pl.BlockSpec((None, 1, pw + LANES), lambda b, g, pt: (page_index(b, g, pt, j), 0, 0))

    new_spec = pl.BlockSpec((None, rows, hd), lambda b, g, pt: (b, 0, 0))
    in_specs = [new_spec, new_spec, new_spec,
                pl.BlockSpec((None, t_new, LANES), lambda b, g, pt: (b, 0, 0))]
    in_specs += [cache_spec(j) for j in range(n_pg)]
    in_specs += [cache_spec(j) for j in range(n_pg)]
    in_specs += [suffix_spec(j) for j in range(n_pg)]
    grid_spec = pltpu.PrefetchScalarGridSpec(
        num_scalar_prefetch=1,
        grid=(nbd, n_pages // n_pg),
        in_specs=in_specs,
        out_specs=new_spec,
        scratch_shapes=[pltpu.VMEM((rows, 1), F32),
                        pltpu.VMEM((rows, 1), F32),
                        pltpu.VMEM((rows, 1), F32),
                        pltpu.VMEM((rows, hd), F32),
                        pltpu.VMEM((1, LANES), F32),
                        pltpu.VMEM((n_pg * pw, hd), BF16),
                        pltpu.VMEM((n_pg * pw, hd), BF16)])
    to_rows = lambda a: a.reshape(nbd, rows, hd)
    out = pl.pallas_call(
        functools.partial(_fox_sample_kernel, n_pg=n_pg, nh=nh),
        grid_spec=grid_spec,
        out_shape=jax.ShapeDtypeStruct((nbd, rows, hd), BF16),
        compiler_params=_params(2),
        name="fox_sample",
    )(page_table, to_rows(q), to_rows(k), to_rows(v), lf.reshape(nbd, t_new, LANES),
      *([cache_k] * n_pg), *([cache_v] * n_pg), *([wt] * n_pg))
    return out.reshape(nbd * t_new, nh * hd)


def _merge_kernel(ya_ref, o_ref, woa_ref, wob_ref, sa_ref, sb_ref, m_ref):
    a = _dot(ya_ref[...], woa_ref[...])
    b = _dot(o_ref[...], wob_ref[...])
    m_ref[...] = (sa_ref[...].astype(F32) * a + sb_ref[...].astype(F32) * b).astype(m_ref.dtype)


def _merge(ya, o, woa, wob, sa, sb):
    m, ca = ya.shape
    cb = o.shape[1]
    n = woa.shape[1]
    tm = _pick(m, (2048, 1024, 512, 256, 128))
    tn = _pick(n, (512, 256, 128))
    gspec = pl.BlockSpec((tm, tn), lambda i, j: (i, j))
    return pl.pallas_call(
        _merge_kernel,
        grid=(m // tm, n // tn),
        in_specs=[pl.BlockSpec((tm, ca), lambda i, j: (i, 0)),
                  pl.BlockSpec((tm, cb), lambda i, j: (i, 0)),
                  pl.BlockSpec((ca, tn), lambda i, j: (0, j)),
                  pl.BlockSpec((cb, tn), lambda i, j: (0, j)), gspec, gspec],
        out_specs=gspec,
        out_shape=jax.ShapeDtypeStruct((m, n), BF16),
        compiler_params=_params(2),
        name="mixer_merge",
    )(ya, o, woa, wob, sa, sb)


def _oproj_kernel(mix_ref, wo_ref, x_ref, g_ref, x1_ref, xn_ref):
    x1 = x_ref[...] + _dot(mix_ref[...], wo_ref[...])
    x1_ref[...] = x1
    xn_ref[...] = _rms_scale(x1, g_ref[...]).astype(xn_ref.dtype)


def _oproj(mix, wo, x, g):
    m, d = x.shape
    tm = _pick(m, (512, 256, 128))
    rspec = pl.BlockSpec((tm, d), lambda i: (i, 0))
    return pl.pallas_call(
        _oproj_kernel,
        grid=(m // tm,),
        in_specs=[rspec, pl.BlockSpec((d, d), lambda i: (0, 0)), rspec,
                  pl.BlockSpec((1, d), lambda i: (0, 0))],
        out_specs=[rspec, rspec],
        out_shape=[jax.ShapeDtypeStruct((m, d), F32), jax.ShapeDtypeStruct((m, d), BF16)],
        compiler_params=_params(1),
        name="oproj_residual_norm",
    )(mix, wo, x, g.reshape(1, d))


def _ffn_up_prompt_kernel(xn_ref, wu_ref, wg_ref, cw_ref, h_ref, tail_ref):
    xn = xn_ref[...]
    gpre = _dot(xn, wg_ref[...])
    tpos = lax.broadcasted_iota(jnp.int32, gpre.shape, 0)
    gc = _causal_conv(gpre, cw_ref[...], tpos, None)
    h_ref[...] = (gc * _sigmoid(gc) * _dot(xn, wu_ref[...])).astype(h_ref.dtype)
    nt = tail_ref.shape[0]
    tail_ref[...] = gpre[gpre.shape[0] - nt:, :]


def _ffn_up_sample_kernel(xn_ref, wu_ref, wg_ref, cw_ref, p1_ref, p2_ref, h_ref, g_ref, *, seq):
    xn = xn_ref[...]
    gpre = _dot(xn, wg_ref[...])
    tpos = lax.broadcasted_iota(jnp.int32, gpre.shape, 0) % seq
    gc = _causal_conv(gpre, cw_ref[...], tpos, [p1_ref[...], p2_ref[...]])
    h_ref[...] = (gc * _sigmoid(gc) * _dot(xn, wu_ref[...])).astype(h_ref.dtype)
    g_ref[...] = gpre


def _ffn_up_prompt(xn, wu, wg, cw, nb, seq):
    m, d = xn.shape
    f = wu.shape[1]
    km1 = cw.shape[0] - 1
    tn = _pick(f, (512, 256, 128))
    wspec = pl.BlockSpec((d, tn), lambda b, j: (0, j))
    return pl.pallas_call(
        _ffn_up_prompt_kernel,
        grid=(nb, f // tn),
        in_specs=[pl.BlockSpec((seq, d), lambda b, j: (b, 0)), wspec, wspec,
                  pl.BlockSpec((km1 + 1, tn), lambda b, j: (0, j))],
        out_specs=[pl.BlockSpec((seq, tn), lambda b, j: (b, j)),
                   pl.BlockSpec((None, km1, tn), lambda b, j: (b, 0, j))],
        out_shape=[jax.ShapeDtypeStruct((m, f), BF16), jax.ShapeDtypeStruct((nb, km1, f), F32)],
        compiler_params=_params(2),
        name="ffn_up_prompt",
    )(xn, wu, wg, cw)


def _ffn_up_sample(xn, wu, wg, cw, state, seq):
    m, d = xn.shape
    f = wu.shape[1]
    tn = _pick(f, (512, 256, 128))
    p1, p2 = _prev_rows(state, seq)
    wspec = pl.BlockSpec((d, tn), lambda j: (0, j))
    cspec = pl.BlockSpec((m, tn), lambda j: (0, j))
    h, gpre = pl.pallas_call(
        functools.partial(_ffn_up_sample_kernel, seq=seq),
        grid=(f // tn,),
        in_specs=[pl.BlockSpec((m, d), lambda j: (0, 0)), wspec, wspec,
                  pl.BlockSpec((cw.shape[0], tn), lambda j: (0, j)), cspec, cspec],
        out_specs=[cspec, cspec],
        out_shape=[jax.ShapeDtypeStruct((m, f), BF16), jax.ShapeDtypeStruct((m, f), F32)],
        compiler_params=_params(1),
        name="ffn_up_sample",
    )(xn, wu, wg, cw, p1, p2)
    return h, _new_state(state, gpre, seq)


def _ffn_down_kernel(h_ref, wd_ref, x1_ref, g_ref, y_ref, *, normalize, nk):
    kk = pl.program_id(1)

    @pl.when(kk == 0)
    def _first():
        y_ref[...] = x1_ref[...]

    y_ref[...] += _dot(h_ref[...], wd_ref[...])

    if normalize:
        @pl.when(kk == nk - 1)
        def _last():
            y_ref[...] = _rms_scale(y_ref[...], g_ref[...])


def _ffn_down(h, wd, x1, g, normalize):
    m, f = h.shape
    d = x1.shape[1]
    tm = _pick(m, (1024, 512, 256, 128))
    tk = _pick(f, (1408, 1024, 512, 256, 128))
    return pl.pallas_call(
        functools.partial(_ffn_down_kernel, normalize=normalize, nk=f // tk),
        grid=(m // tm, f // tk),
        in_specs=[pl.BlockSpec((tm, tk), lambda i, kk: (i, kk)),
                  pl.BlockSpec((tk, d), lambda i, kk: (kk, 0)),
                  pl.BlockSpec((tm, d), lambda i, kk: (i, 0), pipeline_mode=pl.Buffered(1)),
                  pl.BlockSpec((1, d), lambda i, kk: (0, 0))],
        out_specs=pl.BlockSpec((tm, d), lambda i, kk: (i, 0)),
        out_shape=jax.ShapeDtypeStruct((m, d), F32),
        compiler_params=_params(2),
        name="ffn_down_residual_norm",
    )(h, wd, x1, g.reshape(1, d))


def _in_proj_weights(w_in_l, b_f_l, d_conv, d_attn, nh, d_model):
    w = w_in_l.astype(BF16)
    sizes = (d_conv, d_conv, d_conv, d_attn, d_attn, d_attn, nh, d_model, d_model)
    cols, start = [], 0
    for s in sizes:
        cols.append(_Cols(w, start, s))
        start += s
    wf = cols[6]
    wf_pad = jnp.pad(w[:, wf.start:wf.start + nh], ((0, 0), (0, LANES - nh)))
    bf_pad = jnp.pad(b_f_l.astype(F32), (0, LANES - nh)).reshape(1, LANES)
    return cols[:6], cols[7:], wf_pad, bf_pad


def kernel(x_prompt, x_sample, cache_k, cache_v, cache_lf, state_conv_a, state_conv_ffn, page_table,
           norm_mix_g, w_in, b_f, conv_a_w, w_out_a, w_out_b, w_o, norm_ffn_g, w_up, w_gate,
           conv_ffn_w, w_down, norm_final_g):
    bp, tp, d_model = x_prompt.shape
    bd, td, _ = x_sample.shape
    depth = w_in.shape[0]
    nh, hd = cache_k.shape[3], cache_k.shape[4]
    d_attn = nh * hd
    d_conv = conv_a_w.shape[2]
    assert conv_a_w.shape[1] == 3 and conv_ffn_w.shape[1] == 3
    assert nh <= SUBLANES and hd % LANES == 0
    q_scale = hd ** -0.5 * LOG2E

    hp = x_prompt.reshape(bp * tp, d_model)
    hs = x_sample.reshape(bd * td, d_model)
    outs = [[] for _ in range(10)]
    for l in range(depth):
        (wh, wb, wc, wq, wk, wv), (wga, wgb), wf_pad, bf_pad = _in_proj_weights(
            w_in[l], b_f[l], d_conv, d_attn, nh, d_model)
        woa, wob, wo = w_out_a[l].astype(BF16), w_out_b[l].astype(BF16), w_o[l].astype(BF16)
        wu, wg, wd = w_up[l].astype(BF16), w_gate[l].astype(BF16), w_down[l].astype(BF16)
        last = l == depth - 1

        xn = _rmsnorm_bf16(hp, norm_mix_g[l])
        ya, cap = _mixa_prompt(xn, wh, wb, wc, conv_a_w[l], bp, tp)
        q, kp, vp = _qkv(xn, wq, wk, wv, q_scale)
        sa, sb, lfp = _gates(xn, wga, wgb, wf_pad, bf_pad)
        fcum = _cumsum_rows(lfp, bp, tp)
        o = _fox_prompt(q, kp, vp, fcum, bp, tp, nh, hd)
        mix = _merge(ya, o, woa, wob, sa, sb)
        x1, xn2 = _oproj(mix, wo, hp, norm_ffn_g[l])
        hact, cfp = _ffn_up_prompt(xn2, wu, wg, conv_ffn_w[l], bp, tp)
        hp = _ffn_down(hact, wd, x1, norm_final_g, last)

        xn = _rmsnorm_bf16(hs, norm_mix_g[l])
        ya, cas = _mixa_sample(xn, wh, wb, wc, conv_a_w[l], state_conv_a[l], td)
        q, ks, vs = _qkv(xn, wq, wk, wv, q_scale)
        sa, sb, lfs = _gates(xn, wga, wgb, wf_pad, bf_pad)
        wt = _page_suffix(cache_lf[l])
        o = _fox_sample(q, ks, vs, lfs, cache_k, cache_v, l, wt, page_table, bd, td, nh, hd)
        mix = _merge(ya, o, woa, wob, sa, sb)
        x1s, xn2 = _oproj(mix, wo, hs, norm_ffn_g[l])
        hact, cfs = _ffn_up_sample(xn2, wu, wg, conv_ffn_w[l], state_conv_ffn[l], td)
        hs = _ffn_down(hact, wd, x1s, norm_final_g, last)

        for lst, val in zip(outs, (kp.reshape(bp, tp, nh, hd), vp.reshape(bp, tp, nh, hd),
                                   lfp[:, :nh].reshape(bp, tp, nh), ks.reshape(bd, td, nh, hd),
                                   vs.reshape(bd, td, nh, hd), lfs[:, :nh].reshape(bd, td, nh),
                                   cap, cas, cfp, cfs)):
            lst.append(val)
    return (hp.reshape(bp, tp, d_model), hs.reshape(bd, td, d_model), *[jnp.stack(o_) for o_ in outs])
```

```python
import functools
from typing import NamedTuple

import jax
import jax.numpy as jnp
from jax import lax
from jax.experimental import pallas as pl
from jax.experimental.pallas import tpu as pltpu

F32 = jnp.float32
BF16 = jnp.bfloat16
EPS = 1e-6
LANES = 128
SUBLANES = 8
VMEM_LIMIT = 56 * 1024 * 1024
LOG2E = 1.4426950408889634


class _Cols(NamedTuple):
    array: jax.Array
    start: int
    width: int


def _col_operand(cols, tn):
    if cols.start % tn == 0:
        return cols.array, cols.start // tn
    return cols.array[:, cols.start:cols.start + cols.width], 0


def _pick(n, cands):
    for c in cands:
        if n % c == 0:
            return c
    return n


def _params(n_axes):
    return pltpu.CompilerParams(dimension_semantics=("arbitrary",) * n_axes,
                                vmem_limit_bytes=VMEM_LIMIT)


def _dot(a, b):
    return jnp.dot(a, b, preferred_element_type=F32)


def _dot_nt(a, b):
    return lax.dot_general(a, b, (((1,), (1,)), ((), ())), preferred_element_type=F32)


def _split3(x):
    hi = x.astype(BF16)
    r1 = x - hi.astype(F32)
    mid = r1.astype(BF16)
    lo = (r1 - mid.astype(F32)).astype(BF16)
    return hi, mid, lo


def _dot01(ones_mat, x, ones_on_left):
    out = None
    for part in _split3(x):
        t = _dot(ones_mat, part) if ones_on_left else _dot(part, ones_mat)
        out = t if out is None else out + t
    return out


def _sigmoid(x):
    return 1.0 / (1.0 + jnp.exp(-x))


def _log_sigmoid(x):
    return jnp.minimum(x, 0.0) - jnp.log1p(jnp.exp(-jnp.abs(x)))


def _rms_scale(x, g):
    var = jnp.mean(x * x, axis=-1, keepdims=True)
    return x * lax.rsqrt(var + EPS) * g


def _causal_conv(u, cw, tpos, prev):
    k = cw.shape[0]
    y = u * cw[k - 1:k, :]
    for j in range(1, k):
        shifted = pltpu.roll(u, j, axis=0)
        fill = 0.0 if prev is None else prev[j - 1]
        y = y + jnp.where(tpos >= j, shifted, fill) * cw[k - 1 - j:k - j, :]
    return y


def _mixa_prompt_kernel(xn_ref, wh_ref, wb_ref, wc_ref, cw_ref, ya_ref, tail_ref):
    xn = xn_ref[...]
    u = _dot(xn, wc_ref[...]) * _dot(xn, wh_ref[...])
    tpos = lax.broadcasted_iota(jnp.int32, u.shape, 0)
    y = _causal_conv(u, cw_ref[...], tpos, None)
    ya_ref[...] = (_dot(xn, wb_ref[...]) * y).astype(ya_ref.dtype)
    nt = tail_ref.shape[0]
    tail_ref[...] = u[u.shape[0] - nt:, :]


def _mixa_sample_kernel(xn_ref, wh_ref, wb_ref, wc_ref, cw_ref, p1_ref, p2_ref, ya_ref, u_ref, *, seq):
    xn = xn_ref[...]
    u = _dot(xn, wc_ref[...]) * _dot(xn, wh_ref[...])
    tpos = lax.broadcasted_iota(jnp.int32, u.shape, 0) % seq
    y = _causal_conv(u, cw_ref[...], tpos, [p1_ref[...], p2_ref[...]])
    ya_ref[...] = (_dot(xn, wb_ref[...]) * y).astype(ya_ref.dtype)
    u_ref[...] = u


def _prev_rows(state, seq):
    b, km1, c = state.shape
    outs = []
    for j in range(1, km1 + 1):
        rows = [state[:, km1 - j + t, :] if t < j else jnp.zeros((b, c), state.dtype) for t in range(seq)]
        outs.append(jnp.stack(rows, axis=1).reshape(b * seq, c))
    return outs


def _new_state(state, u, seq):
    b, km1, c = state.shape
    ext = jnp.concatenate([state, u.reshape(b, seq, c)], axis=1)
    return ext[:, -km1:, :]


def _mixa_prompt(xn, wh, wb, wc, cw, nb, seq):
    m, d = xn.shape
    c = wh.width
    km1 = cw.shape[0] - 1
    tn = _pick(c, (256, 128))
    (wh, oh), (wb, ob), (wc, oc) = (_col_operand(w, tn) for w in (wh, wb, wc))
    wspec = lambda off: pl.BlockSpec((d, tn), lambda b, j: (0, j + off))
    return pl.pallas_call(
        _mixa_prompt_kernel,
        grid=(nb, c // tn),
        in_specs=[pl.BlockSpec((seq, d), lambda b, j: (b, 0)), wspec(oh), wspec(ob), wspec(oc),
                  pl.BlockSpec((km1 + 1, tn), lambda b, j: (0, j))],
        out_specs=[pl.BlockSpec((seq, tn), lambda b, j: (b, j)),
                   pl.BlockSpec((None, km1, tn), lambda b, j: (b, 0, j))],
        out_shape=[jax.ShapeDtypeStruct((m, c), BF16),
                   jax.ShapeDtypeStruct((nb, km1, c), F32)],
        compiler_params=_params(2),
        name="mixa_prompt",
    )(xn, wh, wb, wc, cw)


def _mixa_sample(xn, wh, wb, wc, cw, state, seq):
    m, d = xn.shape
    c = wh.width
    tn = _pick(c, (512, 256, 128))
    p1, p2 = _prev_rows(state, seq)
    (wh, oh), (wb, ob), (wc, oc) = (_col_operand(w, tn) for w in (wh, wb, wc))
    wspec = lambda off: pl.BlockSpec((d, tn), lambda j: (0, j + off))
    cspec = pl.BlockSpec((m, tn), lambda j: (0, j))
    ya, u = pl.pallas_call(
        functools.partial(_mixa_sample_kernel, seq=seq),
        grid=(c // tn,),
        in_specs=[pl.BlockSpec((m, d), lambda j: (0, 0)), wspec(oh), wspec(ob), wspec(oc),
                  pl.BlockSpec((cw.shape[0], tn), lambda j: (0, j)), cspec, cspec],
        out_specs=[cspec, cspec],
        out_shape=[jax.ShapeDtypeStruct((m, c), BF16), jax.ShapeDtypeStruct((m, c), F32)],
        compiler_params=_params(1),
        name="mixa_sample",
    )(xn, wh, wb, wc, cw, p1, p2)
    return ya, _new_state(state, u, seq)


def _qkv_kernel(xn_ref, wq_ref, wk_ref, wv_ref, q_ref, k_ref, v_ref, *, q_scale):
    xn = xn_ref[...]
    q_ref[...] = (_dot(xn, wq_ref[...]) * q_scale).astype(q_ref.dtype)
    k_ref[...] = _dot(xn, wk_ref[...])
    v_ref[...] = _dot(xn, wv_ref[...])


def _qkv(xn, wq, wk, wv, q_scale):
    m, d = xn.shape
    n = wq.width
    tm = _pick(m, (1024, 512, 256, 128))
    tn = _pick(n, (512, 256, 128))
    (wq, oq), (wk, ok), (wv, ov) = (_col_operand(w, tn) for w in (wq, wk, wv))
    wspec = lambda off: pl.BlockSpec((d, tn), lambda i, j: (0, j + off))
    ospec = pl.BlockSpec((tm, tn), lambda i, j: (i, j))
    return pl.pallas_call(
        functools.partial(_qkv_kernel, q_scale=q_scale),
        grid=(m // tm, n // tn),
        in_specs=[pl.BlockSpec((tm, d), lambda i, j: (i, 0)), wspec(oq), wspec(ok), wspec(ov)],
        out_specs=[ospec, ospec, ospec],
        out_shape=[jax.ShapeDtypeStruct((m, n), BF16), jax.ShapeDtypeStruct((m, n), F32),
                   jax.ShapeDtypeStruct((m, n), F32)],
        compiler_params=_params(2),
        name="qkv_proj",
    )(xn, wq, wk, wv)


def _gates_kernel(x_ref, g_ref, wa_ref, wb_ref, wf_ref, bf_ref, xn_ref, sa_ref, sb_ref, lf_ref):
    @pl.when(pl.program_id(1) == 0)
    def _norm_and_logforget():
        xn_ref[...] = _rms_scale(x_ref[...], g_ref[...]).astype(xn_ref.dtype)
        lf_ref[...] = _log_sigmoid(_dot(xn_ref[...], wf_ref[...]) + bf_ref[...])

    xn = xn_ref[...]
    sa_ref[...] = _sigmoid(_dot(xn, wa_ref[...])).astype(sa_ref.dtype)
    sb_ref[...] = _sigmoid(_dot(xn, wb_ref[...])).astype(sb_ref.dtype)


def _norm_gates(x, g, wa, wb, wf_pad, bf_pad):
    m, d = x.shape
    n = wa.width
    tm = _pick(m, (1024, 512, 256, 128))
    tn = _pick(n, (512, 256, 128))
    (wa, oa), (wb, ob) = (_col_operand(w, tn) for w in (wa, wb))
    wspec = lambda off: pl.BlockSpec((d, tn), lambda i, j: (0, j + off))
    ospec = pl.BlockSpec((tm, tn), lambda i, j: (i, j))
    rspec = pl.BlockSpec((tm, d), lambda i, j: (i, 0))
    return pl.pallas_call(
        _gates_kernel,
        grid=(m // tm, n // tn),
        in_specs=[rspec, pl.BlockSpec((1, d), lambda i, j: (0, 0)), wspec(oa), wspec(ob),
                  pl.BlockSpec((d, LANES), lambda i, j: (0, 0)),
                  pl.BlockSpec((1, LANES), lambda i, j: (0, 0))],
        out_specs=[rspec, ospec, ospec, pl.BlockSpec((tm, LANES), lambda i, j: (i, 0))],
        out_shape=[jax.ShapeDtypeStruct((m, d), BF16), jax.ShapeDtypeStruct((m, n), BF16),
                   jax.ShapeDtypeStruct((m, n), BF16), jax.ShapeDtypeStruct((m, LANES), F32)],
        compiler_params=_params(2),
        name="norm_gate_proj",
    )(x, g.reshape(1, d), wa, wb, wf_pad, bf_pad)


def _cumsum_kernel(lf_ref, f_ref, *, blk):
    t = lf_ref.shape[0]
    r = lax.broadcasted_iota(jnp.int32, (blk, blk), 0)
    c = lax.broadcasted_iota(jnp.int32, (blk, blk), 1)
    lower = (c <= r).astype(BF16)
    carry = jnp.zeros((1, lf_ref.shape[1]), F32)
    for i in range(t // blk):
        y = _dot01(lower, lf_ref[i * blk:(i + 1) * blk, :], True) + carry
        f_ref[i * blk:(i + 1) * blk, :] = y
        carry = y[blk - 1:blk, :]


def _cumsum_rows(lf, nb, seq):
    blk = _pick(seq, (256, 128))
    return pl.pallas_call(
        functools.partial(_cumsum_kernel, blk=blk),
        grid=(nb,),
        in_specs=[pl.BlockSpec((seq, LANES), lambda b: (b, 0))],
        out_specs=pl.BlockSpec((seq, LANES), lambda b: (b, 0)),
        out_shape=jax.ShapeDtypeStruct(lf.shape, F32),
        compiler_params=_params(1),
        name="logforget_cumsum",
    )(lf)


def _fox_prompt_kernel(q_ref, k_ref, v_ref, fcol_ref, frow_ref, o_ref, *, hg, hd, nq):
    hgrp = pl.program_id(1)
    i = pl.program_id(2)
    tq = q_ref.shape[0]
    lane = lax.broadcasted_iota(jnp.int32, fcol_ref.shape, 1)
    cols_of = lambda e: slice(e * hd, (e + 1) * hd)

    def attend(c):
        past = c * tq
        rows = lax.broadcasted_iota(jnp.int32, (tq, tq), 0)
        cols = lax.broadcasted_iota(jnp.int32, (tq, tq), 1)
        for e in range(hg):
            q = q_ref[:, cols_of(e)]
            fq = jnp.sum(jnp.where(lane == hgrp * hg + e, fcol_ref[...], 0.0),
                         axis=-1, keepdims=True) * LOG2E
            kd = k_ref[past:past + tq, cols_of(e)].astype(BF16)
            td = _dot_nt(q, kd) - frow_ref[e:e + 1, past:past + tq] * LOG2E
            td = jnp.where(cols <= rows, td, -jnp.inf)
            mx = jnp.max(td, axis=-1, keepdims=True)
            if c:
                kp = k_ref[0:past, cols_of(e)].astype(BF16)
                tp = _dot_nt(q, kp) - frow_ref[e:e + 1, 0:past] * LOG2E
                mx = jnp.maximum(mx, jnp.max(tp, axis=-1, keepdims=True))
            shift = (mx + fq) - fq
            pd = jnp.exp2(td - shift)
            l = jnp.sum(pd, axis=-1, keepdims=True)
            acc = _dot(pd.astype(BF16), v_ref[past:past + tq, cols_of(e)].astype(BF16))
            if c:
                pp = jnp.exp2(tp - shift)
                l = l + jnp.sum(pp, axis=-1, keepdims=True)
                acc = acc + _dot(pp.astype(BF16), v_ref[0:past, cols_of(e)].astype(BF16))
            o_ref[:, cols_of(e)] = (acc / l).astype(o_ref.dtype)

    for c in range(nq):
        pl.when(i == c)(functools.partial(attend, c))


def _fox_prompt(q, k, v, fcum, nb, seq, nh, hd):
    m = q.shape[0]
    tq = _pick(seq, (512, 256, 128))
    nq = seq // tq
    hg = _pick(nh, (4, 2, 1))
    ng = nh // hg
    frow = fcum[:, :nh].reshape(nb, seq, nh).transpose(0, 2, 1).reshape(nb * ng, hg, seq)
    return pl.pallas_call(
        functools.partial(_fox_prompt_kernel, hg=hg, hd=hd, nq=nq),
        grid=(nb, ng, nq),
        in_specs=[pl.BlockSpec((tq, hg * hd), lambda b, h, i: (b * nq + i, h)),
                  pl.BlockSpec((seq, hg * hd), lambda b, h, i: (b, h)),
                  pl.BlockSpec((seq, hg * hd), lambda b, h, i: (b, h)),
                  pl.BlockSpec((tq, LANES), lambda b, h, i: (b * nq + i, 0)),
                  pl.BlockSpec((None, hg, seq), lambda b, h, i: (b * ng + h, 0, 0))],
        out_specs=pl.BlockSpec((tq, hg * hd), lambda b, h, i: (b * nq + i, h)),
        out_shape=jax.ShapeDtypeStruct((m, nh * hd), BF16),
        compiler_params=_params(3),
        name="fox_prompt",
    )(q, k, v, fcum, frow)


def _suffix_kernel(lf_ref, o_ref, *, nh):
    w = lf_ref.shape[1]
    wo = o_ref.shape[1]
    shift = nh.bit_length() - 1
    r = lax.broadcasted_iota(jnp.int32, (w, wo), 0)
    c = lax.broadcasted_iota(jnp.int32, (w, wo), 1)
    same_head = jnp.bitwise_and(r, nh - 1) == jnp.bitwise_and(c, nh - 1)
    later_key = lax.shift_right_logical(r, shift) > lax.shift_right_logical(c, shift)
    sel = jnp.logical_and(same_head, jnp.logical_or(later_key, c >= w)).astype(BF16)
    o_ref[...] = _dot01(sel, lf_ref[...], False)


def _page_suffix(cache_lf_l):
    n_pool, page, nh = cache_lf_l.shape
    assert nh & (nh - 1) == 0 and LANES % nh == 0
    w = page * nh
    pp = _pick(n_pool, (512, 256, 128, 64, 32, 16, 8))
    out = pl.pallas_call(
        functools.partial(_suffix_kernel, nh=nh),
        grid=(n_pool // pp,),
        in_specs=[pl.BlockSpec((pp, w), lambda i: (i, 0))],
        out_specs=pl.BlockSpec((pp, w + LANES), lambda i: (i, 0)),
        out_shape=jax.ShapeDtypeStruct((n_pool, w + LANES), F32),
        compiler_params=_params(1),
        name="page_suffix",
    )(cache_lf_l.reshape(n_pool, w))
    return out


def _fox_sample_kernel(pt_ref, q_ref, kn_ref, vn_ref, lfn_ref, *rest, n_pg, nh, page_slot):
    k_refs = rest[:n_pg]
    v_refs = rest[n_pg:2 * n_pg]
    w_refs = rest[2 * n_pg:3 * n_pg]
    o_ref = rest[3 * n_pg]
    fn_ref, m_ref, l_ref, acc_ref, tail_ref, kcat_ref, vcat_ref = rest[3 * n_pg + 1:]
    g = pl.program_id(1)
    rows, hd = q_ref.shape
    t_new = rows // nh
    pw = k_refs[0].shape[0] * nh

    @pl.when(g == 0)
    def _init():
        eye = (lax.broadcasted_iota(jnp.int32, (nh, LANES), 0)
               == lax.broadcasted_iota(jnp.int32, (nh, LANES), 1))
        qf = q_ref[...].astype(F32)
        kn = kn_ref[...]
        vn = vn_ref[...]
        lfn = lfn_ref[...]
        fcol = []
        run = jnp.zeros((1, LANES), F32)
        for t in range(t_new):
            run = run + lfn[t:t + 1, :]
            fcol.append(jnp.sum(jnp.where(eye, jnp.broadcast_to(run, (nh, LANES)), 0.0),
                                axis=-1, keepdims=True) * LOG2E)
        for t in range(t_new):
            rs = slice(t * nh, (t + 1) * nh)
            s = [jnp.sum(qf[rs, :] * kn[j * nh:(j + 1) * nh, :], axis=-1, keepdims=True)
                 + (fcol[t] - fcol[j]) for j in range(t + 1)]
            m = s[0]
            for sj in s[1:]:
                m = jnp.maximum(m, sj)
            l = jnp.zeros((nh, 1), F32)
            acc = jnp.zeros((nh, hd), F32)
            for j, sj in enumerate(s):
                p = jnp.exp2(sj - m)
                l = l + p
                acc = acc + p * vn[j * nh:(j + 1) * nh, :]
            fn_ref[rs, :] = fcol[t]
            m_ref[rs, :] = m
            l_ref[rs, :] = l
            acc_ref[rs, :] = acc
        tail_ref[...] = jnp.zeros_like(tail_ref)

    reps = pw // LANES
    tail = tail_ref[...]
    bias = []
    for j in range(n_pg):
        kcat_ref[j * pw:(j + 1) * pw, :] = k_refs[j][...].reshape(pw, hd).astype(kcat_ref.dtype)
        vcat_ref[j * pw:(j + 1) * pw, :] = v_refs[j][...].reshape(pw, hd).astype(vcat_ref.dtype)
        slot = jnp.bitwise_and(page_slot(pt_ref, pl.program_id(0), g, j), SUBLANES - 1)
        w = w_refs[j][pl.ds(slot, 1), :]
        bias.append(w[:, :pw] + jnp.concatenate([tail] * reps, axis=1))
        tail = tail + w[:, pw:]
    tail_ref[...] = tail
    lane = lax.broadcasted_iota(jnp.int32, (rows, LANES), 1)
    row = lax.broadcasted_iota(jnp.int32, (rows, LANES), 0)
    own = jnp.bitwise_and(lane, nh - 1) == jnp.bitwise_and(row, nh - 1)
    mask = jnp.where(own, 0.0, -jnp.inf)
    bias = jnp.concatenate(bias, axis=1) * LOG2E
    t = _dot_nt(q_ref[...], kcat_ref[...])
    t = t + (bias + jnp.concatenate([mask] * (n_pg * reps), axis=1))
    fn = fn_ref[...]
    m_old = m_ref[...]
    m_new = jnp.maximum(m_old, jnp.max(t, axis=-1, keepdims=True) + fn)
    alpha = jnp.exp2(m_old - m_new)
    p = jnp.exp2(t - (m_new - fn))
    l_ref[...] = alpha * l_ref[...] + jnp.sum(p, axis=-1, keepdims=True)
    acc_ref[...] = alpha * acc_ref[...] + _dot(p.astype(BF16), vcat_ref[...])
    m_ref[...] = m_new

    @pl.when(g == pl.num_programs(1) - 1)
    def _finish():
        o_ref[...] = (acc_ref[...] / l_ref[...]).astype(o_ref.dtype)


def _fox_sample(q, k, v, lf, cache_k, cache_v, layer, wt, page_table, nbd, t_new, nh, hd):
    page = cache_k.shape[2]
    assert nh == SUBLANES, "a cached (head, head_dim) slab must be exactly one f32 vreg tile"
    n_pages = page_table.shape[1]
    n_pg = _pick(n_pages, (16, 8, 4, 2, 1))
    rows = t_new * nh
    pw = page * nh

    assert wt.shape[0] % SUBLANES == 0

    def page_index(pt, b, g, j):
        return pt[b, n_pages - 1 - (g * n_pg + j)]

    def cache_spec(j):
        return pl.BlockSpec((None, None, page, nh, hd),
                            lambda b, g, pt: (layer, page_index(pt, b, g, j), 0, 0, 0))

    def suffix_spec(j):
        return pl.BlockSpec((SUBLANES, pw + LANES),
                            lambda b, g, pt: (lax.shift_right_logical(
                                page_index(pt, b, g, j), SUBLANES.bit_length() - 1), 0))

    new_spec = pl.BlockSpec((None, rows, hd), lambda b, g, pt: (b, 0, 0))
    in_specs = [new_spec, new_spec, new_spec,
                pl.BlockSpec((None, t_new, LANES), lambda b, g, pt: (b, 0, 0))]
    in_specs += [cache_spec(j) for j in range(n_pg)]
    in_specs += [cache_spec(j) for j in range(n_pg)]
    in_specs += [suffix_spec(j) for j in range(n_pg)]
    grid_spec = pltpu.PrefetchScalarGridSpec(
        num_scalar_prefetch=1,
        grid=(nbd, n_pages // n_pg),
        in_specs=in_specs,
        out_specs=new_spec,
        scratch_shapes=[pltpu.VMEM((rows, 1), F32),
                        pltpu.VMEM((rows, 1), F32),
                        pltpu.VMEM((rows, 1), F32),
                        pltpu.VMEM((rows, hd), F32),
                        pltpu.VMEM((1, LANES), F32),
                        pltpu.VMEM((n_pg * pw, hd), BF16),
                        pltpu.VMEM((n_pg * pw, hd), BF16)])
    to_rows = lambda a: a.reshape(nbd, rows, hd)
    out = pl.pallas_call(
        functools.partial(_fox_sample_kernel, n_pg=n_pg, nh=nh, page_slot=page_index),
        grid_spec=grid_spec,
        out_shape=jax.ShapeDtypeStruct((nbd, rows, hd), BF16),
        compiler_params=_params(2),
        name="fox_sample",
    )(page_table, to_rows(q), to_rows(k), to_rows(v), lf.reshape(nbd, t_new, LANES),
      *([cache_k] * n_pg), *([cache_v] * n_pg), *([wt] * n_pg))
    return out.reshape(nbd * t_new, nh * hd)


def _merge_kernel(ya_ref, o_ref, woa_ref, wob_ref, sa_ref, sb_ref, m_ref):
    a = _dot(ya_ref[...], woa_ref[...])
    b = _dot(o_ref[...], wob_ref[...])
    m_ref[...] = (sa_ref[...].astype(F32) * a + sb_ref[...].astype(F32) * b).astype(m_ref.dtype)


def _merge(ya, o, woa, wob, sa, sb):
    m, ca = ya.shape
    cb = o.shape[1]
    n = woa.shape[1]
    tm = _pick(m, (2048, 1024, 512, 256, 128))
    tn = _pick(n, (512, 256, 128))
    gspec = pl.BlockSpec((tm, tn), lambda i, j: (i, j))
    return pl.pallas_call(
        _merge_kernel,
        grid=(m // tm, n // tn),
        in_specs=[pl.BlockSpec((tm, ca), lambda i, j: (i, 0)),
                  pl.BlockSpec((tm, cb), lambda i, j: (i, 0)),
                  pl.BlockSpec((ca, tn), lambda i, j: (0, j)),
                  pl.BlockSpec((cb, tn), lambda i, j: (0, j)), gspec, gspec],
        out_specs=gspec,
        out_shape=jax.ShapeDtypeStruct((m, n), BF16),
        compiler_params=_params(2),
        name="mixer_merge",
    )(ya, o, woa, wob, sa, sb)


def _oproj_kernel(mix_ref, wo_ref, x_ref, g_ref, x1_ref, xn_ref):
    x1 = x_ref[...] + _dot(mix_ref[...], wo_ref[...])
    x1_ref[...] = x1
    xn_ref[...] = _rms_scale(x1, g_ref[...]).astype(xn_ref.dtype)


def _oproj(mix, wo, x, g):
    m, d = x.shape
    tm = _pick(m, (512, 256, 128))
    rspec = pl.BlockSpec((tm, d), lambda i: (i, 0))
    return pl.pallas_call(
        _oproj_kernel,
        grid=(m // tm,),
        in_specs=[rspec, pl.BlockSpec((d, d), lambda i: (0, 0)), rspec,
                  pl.BlockSpec((1, d), lambda i: (0, 0))],
        out_specs=[rspec, rspec],
        out_shape=[jax.ShapeDtypeStruct((m, d), F32), jax.ShapeDtypeStruct((m, d), BF16)],
        compiler_params=_params(1),
        name="oproj_residual_norm",
    )(mix, wo, x, g.reshape(1, d))


def _ffn_up_prompt_kernel(xn_ref, wu_ref, wg_ref, cw_ref, h_ref, tail_ref):
    xn = xn_ref[...]
    gpre = _dot(xn, wg_ref[...])
    tpos = lax.broadcasted_iota(jnp.int32, gpre.shape, 0)
    gc = _causal_conv(gpre, cw_ref[...], tpos, None)
    h_ref[...] = (gc * _sigmoid(gc) * _dot(xn, wu_ref[...])).astype(h_ref.dtype)
    nt = tail_ref.shape[0]
    tail_ref[...] = gpre[gpre.shape[0] - nt:, :]


def _ffn_up_sample_kernel(xn_ref, wu_ref, wg_ref, cw_ref, p1_ref, p2_ref, h_ref, g_ref, *, seq):
    xn = xn_ref[...]
    gpre = _dot(xn, wg_ref[...])
    tpos = lax.broadcasted_iota(jnp.int32, gpre.shape, 0) % seq
    gc = _causal_conv(gpre, cw_ref[...], tpos, [p1_ref[...], p2_ref[...]])
    h_ref[...] = (gc * _sigmoid(gc) * _dot(xn, wu_ref[...])).astype(h_ref.dtype)
    g_ref[...] = gpre


def _ffn_up_prompt(xn, wu, wg, cw, nb, seq):
    m, d = xn.shape
    f = wu.shape[1]
    km1 = cw.shape[0] - 1
    tn = _pick(f, (512, 256, 128))
    wspec = pl.BlockSpec((d, tn), lambda b, j: (0, j))
    return pl.pallas_call(
        _ffn_up_prompt_kernel,
        grid=(nb, f // tn),
        in_specs=[pl.BlockSpec((seq, d), lambda b, j: (b, 0)), wspec, wspec,
                  pl.BlockSpec((km1 + 1, tn), lambda b, j: (0, j))],
        out_specs=[pl.BlockSpec((seq, tn), lambda b, j: (b, j)),
                   pl.BlockSpec((None, km1, tn), lambda b, j: (b, 0, j))],
        out_shape=[jax.ShapeDtypeStruct((m, f), BF16), jax.ShapeDtypeStruct((nb, km1, f), F32)],
        compiler_params=_params(2),
        name="ffn_up_prompt",
    )(xn, wu, wg, cw)


def _ffn_up_sample(xn, wu, wg, cw, state, seq):
    m, d = xn.shape
    f = wu.shape[1]
    tn = _pick(f, (512, 256, 128))
    p1, p2 = _prev_rows(state, seq)
    wspec = pl.BlockSpec((d, tn), lambda j: (0, j))
    cspec = pl.BlockSpec((m, tn), lambda j: (0, j))
    h, gpre = pl.pallas_call(
        functools.partial(_ffn_up_sample_kernel, seq=seq),
        grid=(f // tn,),
        in_specs=[pl.BlockSpec((m, d), lambda j: (0, 0)), wspec, wspec,
                  pl.BlockSpec((cw.shape[0], tn), lambda j: (0, j)), cspec, cspec],
        out_specs=[cspec, cspec],
        out_shape=[jax.ShapeDtypeStruct((m, f), BF16), jax.ShapeDtypeStruct((m, f), F32)],
        compiler_params=_params(1),
        name="ffn_up_sample",
    )(xn, wu, wg, cw, p1, p2)
    return h, _new_state(state, gpre, seq)


def _ffn_down_kernel(h_ref, wd_ref, x1_ref, g_ref, y_ref, *, normalize, nk):
    kk = pl.program_id(1)

    @pl.when(kk == 0)
    def _first():
        y_ref[...] = x1_ref[...]

    y_ref[...] += _dot(h_ref[...], wd_ref[...])

    if normalize:
        @pl.when(kk == nk - 1)
        def _last():
            y_ref[...] = _rms_scale(y_ref[...], g_ref[...])


def _ffn_down(h, wd, x1, g, normalize):
    m, f = h.shape
    d = x1.shape[1]
    tm = _pick(m, (256, 128))
    tk = f
    return pl.pallas_call(
        functools.partial(_ffn_down_kernel, normalize=normalize, nk=f // tk),
        grid=(m // tm, f // tk),
        in_specs=[pl.BlockSpec((tm, tk), lambda i, kk: (i, kk)),
                  pl.BlockSpec((tk, d), lambda i, kk: (kk, 0), pipeline_mode=pl.Buffered(1)),
                  pl.BlockSpec((tm, d), lambda i, kk: (i, 0), pipeline_mode=pl.Buffered(1)),
                  pl.BlockSpec((1, d), lambda i, kk: (0, 0))],
        out_specs=pl.BlockSpec((tm, d), lambda i, kk: (i, 0)),
        out_shape=jax.ShapeDtypeStruct((m, d), F32),
        compiler_params=_params(2),
        name="ffn_down_residual_norm",
    )(h, wd, x1, g.reshape(1, d))


def _in_proj_weights(w_in_l, b_f_l, d_conv, d_attn, nh, d_model):
    w = w_in_l.astype(BF16)
    sizes = (d_conv, d_conv, d_conv, d_attn, d_attn, d_attn, nh, d_model, d_model)
    cols, start = [], 0
    for s in sizes:
        cols.append(_Cols(w, start, s))
        start += s
    wf = cols[6]
    wf_pad = jnp.pad(w[:, wf.start:wf.start + nh], ((0, 0), (0, LANES - nh)))
    bf_pad = jnp.pad(b_f_l.astype(F32), (0, LANES - nh)).reshape(1, LANES)
    return cols[:6], cols[7:], wf_pad, bf_pad


def kernel(x_prompt, x_sample, cache_k, cache_v, cache_lf, state_conv_a, state_conv_ffn, page_table,
           norm_mix_g, w_in, b_f, conv_a_w, w_out_a, w_out_b, w_o, norm_ffn_g, w_up, w_gate,
           conv_ffn_w, w_down, norm_final_g):
    bp, tp, d_model = x_prompt.shape
    bd, td, _ = x_sample.shape
    depth = w_in.shape[0]
    nh, hd = cache_k.shape[3], cache_k.shape[4]
    d_attn = nh * hd
    d_conv = conv_a_w.shape[2]
    assert conv_a_w.shape[1] == 3 and conv_ffn_w.shape[1] == 3
    assert nh <= SUBLANES and hd % LANES == 0
    q_scale = hd ** -0.5 * LOG2E

    hp = x_prompt.reshape(bp * tp, d_model)
    hs = x_sample.reshape(bd * td, d_model)
    outs = [[] for _ in range(10)]
    for l in range(depth):
        (wh, wb, wc, wq, wk, wv), (wga, wgb), wf_pad, bf_pad = _in_proj_weights(
            w_in[l], b_f[l], d_conv, d_attn, nh, d_model)
        woa, wob, wo = w_out_a[l].astype(BF16), w_out_b[l].astype(BF16), w_o[l].astype(BF16)
        wu, wg, wd = w_up[l].astype(BF16), w_gate[l].astype(BF16), w_down[l].astype(BF16)
        last = l == depth - 1

        xn, sa, sb, lfp = _norm_gates(hp, norm_mix_g[l], wga, wgb, wf_pad, bf_pad)
        ya, cap = _mixa_prompt(xn, wh, wb, wc, conv_a_w[l], bp, tp)
        q, kp, vp = _qkv(xn, wq, wk, wv, q_scale)
        fcum = _cumsum_rows(lfp, bp, tp)
        o = _fox_prompt(q, kp, vp, fcum, bp, tp, nh, hd)
        mix = _merge(ya, o, woa, wob, sa, sb)
        x1, xn2 = _oproj(mix, wo, hp, norm_ffn_g[l])
        hact, cfp = _ffn_up_prompt(xn2, wu, wg, conv_ffn_w[l], bp, tp)
        hp = _ffn_down(hact, wd, x1, norm_final_g, last)

        xn, sa, sb, lfs = _norm_gates(hs, norm_mix_g[l], wga, wgb, wf_pad, bf_pad)
        ya, cas = _mixa_sample(xn, wh, wb, wc, conv_a_w[l], state_conv_a[l], td)
        q, ks, vs = _qkv(xn, wq, wk, wv, q_scale)
        wt = _page_suffix(cache_lf[l])
        o = _fox_sample(q, ks, vs, lfs, cache_k, cache_v, l, wt, page_table, bd, td, nh, hd)
        mix = _merge(ya, o, woa, wob, sa, sb)
        x1s, xn2 = _oproj(mix, wo, hs, norm_ffn_g[l])
        hact, cfs = _ffn_up_sample(xn2, wu, wg, conv_ffn_w[l], state_conv_ffn[l], td)
        hs = _ffn_down(hact, wd, x1s, norm_final_g, last)

        for lst, val in zip(outs, (kp.reshape(bp, tp, nh, hd), vp.reshape(bp, tp, nh, hd),
                                   lfp[:, :nh].reshape(bp, tp, nh), ks.reshape(bd, td, nh, hd),
                                   vs.reshape(bd, td, nh, hd), lfs[:, :nh].reshape(bd, td, nh),
                                   cap, cas, cfp, cfs)):
            lst.append(val)
    return (hp.reshape(bp, tp, d_model), hs.reshape(bd, td, d_model), *[jnp.stack(o_) for o_ in outs])
```

```python
import functools
from typing import NamedTuple

import jax
import jax.numpy as jnp
from jax import lax
from jax.experimental import pallas as pl
from jax.experimental.pallas import tpu as pltpu

F32 = jnp.float32
BF16 = jnp.bfloat16
EPS = 1e-6
LANES = 128
SUBLANES = 8
VMEM_LIMIT = 56 * 1024 * 1024
LOG2E = 1.4426950408889634


class _Cols(NamedTuple):
    array: jax.Array
    start: int
    width: int


def _col_operand(cols, tn):
    if cols.start % tn == 0:
        return cols.array, cols.start // tn
    return cols.array[:, cols.start:cols.start + cols.width], 0


def _pick(n, cands):
    for c in cands:
        if n % c == 0:
            return c
    return n


def _params(n_axes):
    return pltpu.CompilerParams(dimension_semantics=("arbitrary",) * n_axes,
                                vmem_limit_bytes=VMEM_LIMIT)


def _dot(a, b):
    return jnp.dot(a, b, preferred_element_type=F32)


def _dot_nt(a, b):
    return lax.dot_general(a, b, (((1,), (1,)), ((), ())), preferred_element_type=F32)


def _split3(x):
    hi = x.astype(BF16)
    r1 = x - hi.astype(F32)
    mid = r1.astype(BF16)
    lo = (r1 - mid.astype(F32)).astype(BF16)
    return hi, mid, lo


def _dot01(ones_mat, x, ones_on_left):
    out = None
    for part in _split3(x):
        t = _dot(ones_mat, part) if ones_on_left else _dot(part, ones_mat)
        out = t if out is None else out + t
    return out


def _sigmoid(x):
    return 1.0 / (1.0 + jnp.exp(-x))


def _log_sigmoid(x):
    return jnp.minimum(x, 0.0) - jnp.log1p(jnp.exp(-jnp.abs(x)))


def _rms_scale(x, g):
    var = jnp.mean(x * x, axis=-1, keepdims=True)
    return x * lax.rsqrt(var + EPS) * g


def _causal_conv(u, cw, tpos, prev):
    k = cw.shape[0]
    y = u * cw[k - 1:k, :]
    for j in range(1, k):
        shifted = pltpu.roll(u, j, axis=0)
        fill = 0.0 if prev is None else prev[j - 1]
        y = y + jnp.where(tpos >= j, shifted, fill) * cw[k - 1 - j:k - j, :]
    return y


def _mixa_prompt_kernel(xn_ref, wh_ref, wb_ref, wc_ref, cw_ref, ya_ref, tail_ref):
    xn = xn_ref[...]
    u = _dot(xn, wc_ref[...]) * _dot(xn, wh_ref[...])
    tpos = lax.broadcasted_iota(jnp.int32, u.shape, 0)
    y = _causal_conv(u, cw_ref[...], tpos, None)
    ya_ref[...] = (_dot(xn, wb_ref[...]) * y).astype(ya_ref.dtype)
    nt = tail_ref.shape[0]
    tail_ref[...] = u[u.shape[0] - nt:, :]


def _mixa_sample_kernel(xn_ref, wh_ref, wb_ref, wc_ref, cw_ref, p1_ref, p2_ref, ya_ref, u_ref, *, seq):
    xn = xn_ref[...]
    u = _dot(xn, wc_ref[...]) * _dot(xn, wh_ref[...])
    tpos = lax.broadcasted_iota(jnp.int32, u.shape, 0) % seq
    y = _causal_conv(u, cw_ref[...], tpos, [p1_ref[...], p2_ref[...]])
    ya_ref[...] = (_dot(xn, wb_ref[...]) * y).astype(ya_ref.dtype)
    u_ref[...] = u


def _prev_rows(state, seq):
    b, km1, c = state.shape
    outs = []
    for j in range(1, km1 + 1):
        rows = [state[:, km1 - j + t, :] if t < j else jnp.zeros((b, c), state.dtype) for t in range(seq)]
        outs.append(jnp.stack(rows, axis=1).reshape(b * seq, c))
    return outs


def _new_state(state, u, seq):
    b, km1, c = state.shape
    ext = jnp.concatenate([state, u.reshape(b, seq, c)], axis=1)
    return ext[:, -km1:, :]


def _mixa_prompt(xn, wh, wb, wc, cw, nb, seq):
    m, d = xn.shape
    c = wh.width
    km1 = cw.shape[0] - 1
    tn = _pick(c, (256, 128))
    (wh, oh), (wb, ob), (wc, oc) = (_col_operand(w, tn) for w in (wh, wb, wc))
    wspec = lambda off: pl.BlockSpec((d, tn), lambda b, j: (0, j + off))
    return pl.pallas_call(
        _mixa_prompt_kernel,
        grid=(nb, c // tn),
        in_specs=[pl.BlockSpec((seq, d), lambda b, j: (b, 0)), wspec(oh), wspec(ob), wspec(oc),
                  pl.BlockSpec((km1 + 1, tn), lambda b, j: (0, j))],
        out_specs=[pl.BlockSpec((seq, tn), lambda b, j: (b, j)),
                   pl.BlockSpec((None, km1, tn), lambda b, j: (b, 0, j))],
        out_shape=[jax.ShapeDtypeStruct((m, c), BF16),
                   jax.ShapeDtypeStruct((nb, km1, c), F32)],
        compiler_params=_params(2),
        name="mixa_prompt",
    )(xn, wh, wb, wc, cw)


def _mixa_sample(xn, wh, wb, wc, cw, state, seq):
    m, d = xn.shape
    c = wh.width
    tn = _pick(c, (512, 256, 128))
    p1, p2 = _prev_rows(state, seq)
    (wh, oh), (wb, ob), (wc, oc) = (_col_operand(w, tn) for w in (wh, wb, wc))
    wspec = lambda off: pl.BlockSpec((d, tn), lambda j: (0, j + off))
    cspec = pl.BlockSpec((m, tn), lambda j: (0, j))
    ya, u = pl.pallas_call(
        functools.partial(_mixa_sample_kernel, seq=seq),
        grid=(c // tn,),
        in_specs=[pl.BlockSpec((m, d), lambda j: (0, 0)), wspec(oh), wspec(ob), wspec(oc),
                  pl.BlockSpec((cw.shape[0], tn), lambda j: (0, j)), cspec, cspec],
        out_specs=[cspec, cspec],
        out_shape=[jax.ShapeDtypeStruct((m, c), BF16), jax.ShapeDtypeStruct((m, c), F32)],
        compiler_params=_params(1),
        name="mixa_sample",
    )(xn, wh, wb, wc, cw, p1, p2)
    return ya, _new_state(state, u, seq)


def _qkv_kernel(xn_ref, wq_ref, wk_ref, wv_ref, q_ref, k_ref, v_ref, *, q_scale):
    xn = xn_ref[...]
    q_ref[...] = (_dot(xn, wq_ref[...]) * q_scale).astype(q_ref.dtype)
    k_ref[...] = _dot(xn, wk_ref[...])
    v_ref[...] = _dot(xn, wv_ref[...])


def _qkv(xn, wq, wk, wv, q_scale):
    m, d = xn.shape
    n = wq.width
    tm = _pick(m, (1024, 512, 256, 128))
    tn = _pick(n, (512, 256, 128))
    (wq, oq), (wk, ok), (wv, ov) = (_col_operand(w, tn) for w in (wq, wk, wv))
    wspec = lambda off: pl.BlockSpec((d, tn), lambda i, j: (0, j + off))
    ospec = pl.BlockSpec((tm, tn), lambda i, j: (i, j))
    return pl.pallas_call(
        functools.partial(_qkv_kernel, q_scale=q_scale),
        grid=(m // tm, n // tn),
        in_specs=[pl.BlockSpec((tm, d), lambda i, j: (i, 0)), wspec(oq), wspec(ok), wspec(ov)],
        out_specs=[ospec, ospec, ospec],
        out_shape=[jax.ShapeDtypeStruct((m, n), BF16), jax.ShapeDtypeStruct((m, n), F32),
                   jax.ShapeDtypeStruct((m, n), F32)],
        compiler_params=_params(2),
        name="qkv_proj",
    )(xn, wq, wk, wv)


def _gates_kernel(x_ref, g_ref, wa_ref, wb_ref, wf_ref, bf_ref, xn_ref, sa_ref, sb_ref, lf_ref):
    @pl.when(pl.program_id(1) == 0)
    def _norm_and_logforget():
        xn_ref[...] = _rms_scale(x_ref[...], g_ref[...]).astype(xn_ref.dtype)
        lf_ref[...] = _log_sigmoid(_dot(xn_ref[...], wf_ref[...]) + bf_ref[...])

    xn = xn_ref[...]
    sa_ref[...] = _sigmoid(_dot(xn, wa_ref[...])).astype(sa_ref.dtype)
    sb_ref[...] = _sigmoid(_dot(xn, wb_ref[...])).astype(sb_ref.dtype)


def _norm_gates(x, g, wa, wb, wf_pad, bf_pad):
    m, d = x.shape
    n = wa.width
    tm = _pick(m, (1024, 512, 256, 128))
    tn = _pick(n, (512, 256, 128))
    (wa, oa), (wb, ob) = (_col_operand(w, tn) for w in (wa, wb))
    wspec = lambda off: pl.BlockSpec((d, tn), lambda i, j: (0, j + off))
    ospec = pl.BlockSpec((tm, tn), lambda i, j: (i, j))
    rspec = pl.BlockSpec((tm, d), lambda i, j: (i, 0))
    return pl.pallas_call(
        _gates_kernel,
        grid=(m // tm, n // tn),
        in_specs=[rspec, pl.BlockSpec((1, d), lambda i, j: (0, 0)), wspec(oa), wspec(ob),
                  pl.BlockSpec((d, LANES), lambda i, j: (0, 0)),
                  pl.BlockSpec((1, LANES), lambda i, j: (0, 0))],
        out_specs=[rspec, ospec, ospec, pl.BlockSpec((tm, LANES), lambda i, j: (i, 0))],
        out_shape=[jax.ShapeDtypeStruct((m, d), BF16), jax.ShapeDtypeStruct((m, n), BF16),
                   jax.ShapeDtypeStruct((m, n), BF16), jax.ShapeDtypeStruct((m, LANES), F32)],
        compiler_params=_params(2),
        name="norm_gate_proj",
    )(x, g.reshape(1, d), wa, wb, wf_pad, bf_pad)


def _cumsum_kernel(lf_ref, f_ref, *, blk):
    t = lf_ref.shape[0]
    r = lax.broadcasted_iota(jnp.int32, (blk, blk), 0)
    c = lax.broadcasted_iota(jnp.int32, (blk, blk), 1)
    lower = (c <= r).astype(BF16)
    carry = jnp.zeros((1, lf_ref.shape[1]), F32)
    for i in range(t // blk):
        y = _dot01(lower, lf_ref[i * blk:(i + 1) * blk, :], True) + carry
        f_ref[i * blk:(i + 1) * blk, :] = y
        carry = y[blk - 1:blk, :]


def _cumsum_rows(lf, nb, seq):
    blk = _pick(seq, (256, 128))
    return pl.pallas_call(
        functools.partial(_cumsum_kernel, blk=blk),
        grid=(nb,),
        in_specs=[pl.BlockSpec((seq, LANES), lambda b: (b, 0))],
        out_specs=pl.BlockSpec((seq, LANES), lambda b: (b, 0)),
        out_shape=jax.ShapeDtypeStruct(lf.shape, F32),
        compiler_params=_params(1),
        name="logforget_cumsum",
    )(lf)


def _fox_prompt_kernel(q_ref, k_ref, v_ref, fcol_ref, frow_ref, o_ref, *, hg, hd, nq):
    hgrp = pl.program_id(1)
    i = pl.program_id(2)
    tq = q_ref.shape[0]
    lane = lax.broadcasted_iota(jnp.int32, fcol_ref.shape, 1)
    cols_of = lambda e: slice(e * hd, (e + 1) * hd)

    def attend(c):
        past = c * tq
        rows = lax.broadcasted_iota(jnp.int32, (tq, tq), 0)
        cols = lax.broadcasted_iota(jnp.int32, (tq, tq), 1)
        for e in range(hg):
            q = q_ref[:, cols_of(e)]
            fq = jnp.sum(jnp.where(lane == hgrp * hg + e, fcol_ref[...], 0.0),
                         axis=-1, keepdims=True) * LOG2E
            kd = k_ref[past:past + tq, cols_of(e)].astype(BF16)
            td = _dot_nt(q, kd) - frow_ref[e:e + 1, past:past + tq] * LOG2E
            td = jnp.where(cols <= rows, td, -jnp.inf)
            mx = jnp.max(td, axis=-1, keepdims=True)
            if c:
                kp = k_ref[0:past, cols_of(e)].astype(BF16)
                tp = _dot_nt(q, kp) - frow_ref[e:e + 1, 0:past] * LOG2E
                mx = jnp.maximum(mx, jnp.max(tp, axis=-1, keepdims=True))
            shift = (mx + fq) - fq
            pd = jnp.exp2(td - shift)
            l = jnp.sum(pd, axis=-1, keepdims=True)
            acc = _dot(pd.astype(BF16), v_ref[past:past + tq, cols_of(e)].astype(BF16))
            if c:
                pp = jnp.exp2(tp - shift)
                l = l + jnp.sum(pp, axis=-1, keepdims=True)
                acc = acc + _dot(pp.astype(BF16), v_ref[0:past, cols_of(e)].astype(BF16))
            o_ref[:, cols_of(e)] = (acc / l).astype(o_ref.dtype)

    for c in range(nq):
        pl.when(i == c)(functools.partial(attend, c))


def _fox_prompt(q, k, v, fcum, nb, seq, nh, hd):
    m = q.shape[0]
    tq = _pick(seq, (512, 256, 128))
    nq = seq // tq
    hg = _pick(nh, (4, 2, 1))
    ng = nh // hg
    frow = fcum[:, :nh].reshape(nb, seq, nh).transpose(0, 2, 1).reshape(nb * ng, hg, seq)
    return pl.pallas_call(
        functools.partial(_fox_prompt_kernel, hg=hg, hd=hd, nq=nq),
        grid=(nb, ng, nq),
        in_specs=[pl.BlockSpec((tq, hg * hd), lambda b, h, i: (b * nq + i, h)),
                  pl.BlockSpec((seq, hg * hd), lambda b, h, i: (b, h)),
                  pl.BlockSpec((seq, hg * hd), lambda b, h, i: (b, h)),
                  pl.BlockSpec((tq, LANES), lambda b, h, i: (b * nq + i, 0)),
                  pl.BlockSpec((None, hg, seq), lambda b, h, i: (b * ng + h, 0, 0))],
        out_specs=pl.BlockSpec((tq, hg * hd), lambda b, h, i: (b * nq + i, h)),
        out_shape=jax.ShapeDtypeStruct((m, nh * hd), BF16),
        compiler_params=_params(3),
        name="fox_prompt",
    )(q, k, v, fcum, frow)


def _suffix_kernel(lf_ref, o_ref, *, nh):
    w = lf_ref.shape[1]
    wo = o_ref.shape[1]
    shift = nh.bit_length() - 1
    r = lax.broadcasted_iota(jnp.int32, (w, wo), 0)
    c = lax.broadcasted_iota(jnp.int32, (w, wo), 1)
    same_head = jnp.bitwise_and(r, nh - 1) == jnp.bitwise_and(c, nh - 1)
    later_key = lax.shift_right_logical(r, shift) > lax.shift_right_logical(c, shift)
    sel = jnp.logical_and(same_head, jnp.logical_or(later_key, c >= w)).astype(BF16)
    o_ref[...] = _dot01(sel, lf_ref[...], False)


def _page_suffix(cache_lf_l):
    n_pool, page, nh = cache_lf_l.shape
    assert nh & (nh - 1) == 0 and LANES % nh == 0
    w = page * nh
    pp = _pick(n_pool, (512, 256, 128, 64, 32, 16, 8))
    out = pl.pallas_call(
        functools.partial(_suffix_kernel, nh=nh),
        grid=(n_pool // pp,),
        in_specs=[pl.BlockSpec((pp, w), lambda i: (i, 0))],
        out_specs=pl.BlockSpec((pp, w + LANES), lambda i: (i, 0)),
        out_shape=jax.ShapeDtypeStruct((n_pool, w + LANES), F32),
        compiler_params=_params(1),
        name="page_suffix",
    )(cache_lf_l.reshape(n_pool, w))
    return out


def _fox_sample_kernel(pt_ref, q_ref, kn_ref, vn_ref, lfn_ref, *rest, n_pg, nh, page_slot):
    k_refs = rest[:n_pg]
    v_refs = rest[n_pg:2 * n_pg]
    w_refs = rest[2 * n_pg:3 * n_pg]
    o_ref = rest[3 * n_pg]
    fn_ref, m_ref, l_ref, acc_ref, tail_ref, kcat_ref, vcat_ref = rest[3 * n_pg + 1:]
    g = pl.program_id(1)
    rows, hd = q_ref.shape
    t_new = rows // nh
    pw = k_refs[0].shape[0] * nh

    @pl.when(g == 0)
    def _init():
        eye = (lax.broadcasted_iota(jnp.int32, (nh, LANES), 0)
               == lax.broadcasted_iota(jnp.int32, (nh, LANES), 1))
        qf = q_ref[...].astype(F32)
        kn = kn_ref[...]
        vn = vn_ref[...]
        lfn = lfn_ref[...]
        fcol = []
        run = jnp.zeros((1, LANES), F32)
        for t in range(t_new):
            run = run + lfn[t:t + 1, :]
            fcol.append(jnp.sum(jnp.where(eye, jnp.broadcast_to(run, (nh, LANES)), 0.0),
                                axis=-1, keepdims=True) * LOG2E)
        for t in range(t_new):
            rs = slice(t * nh, (t + 1) * nh)
            s = [jnp.sum(qf[rs, :] * kn[j * nh:(j + 1) * nh, :], axis=-1, keepdims=True)
                 + (fcol[t] - fcol[j]) for j in range(t + 1)]
            m = s[0]
            for sj in s[1:]:
                m = jnp.maximum(m, sj)
            l = jnp.zeros((nh, 1), F32)
            acc = jnp.zeros((nh, hd), F32)
            for j, sj in enumerate(s):
                p = jnp.exp2(sj - m)
                l = l + p
                acc = acc + p * vn[j * nh:(j + 1) * nh, :]
            fn_ref[rs, :] = fcol[t]
            m_ref[rs, :] = m
            l_ref[rs, :] = l
            acc_ref[rs, :] = acc
        tail_ref[...] = jnp.zeros_like(tail_ref)

    reps = pw // LANES
    tail = tail_ref[...]
    bias = []
    for j in range(n_pg):
        kcat_ref[j * pw:(j + 1) * pw, :] = k_refs[j][...].reshape(pw, hd).astype(kcat_ref.dtype)
        vcat_ref[j * pw:(j + 1) * pw, :] = v_refs[j][...].reshape(pw, hd).astype(vcat_ref.dtype)
        slot = jnp.bitwise_and(page_slot(pt_ref, pl.program_id(0), g, j), SUBLANES - 1)
        w = w_refs[j][pl.ds(slot, 1), :]
        bias.append(w[:, :pw] + jnp.concatenate([tail] * reps, axis=1))
        tail = tail + w[:, pw:]
    tail_ref[...] = tail
    lane = lax.broadcasted_iota(jnp.int32, (rows, LANES), 1)
    row = lax.broadcasted_iota(jnp.int32, (rows, LANES), 0)
    own = jnp.bitwise_and(lane, nh - 1) == jnp.bitwise_and(row, nh - 1)
    mask = jnp.where(own, 0.0, -jnp.inf)
    bias = jnp.concatenate(bias, axis=1) * LOG2E
    t = _dot_nt(q_ref[...], kcat_ref[...])
    t = t + (bias + jnp.concatenate([mask] * (n_pg * reps), axis=1))
    fn = fn_ref[...]
    m_old = m_ref[...]
    m_new = jnp.maximum(m_old, jnp.max(t, axis=-1, keepdims=True) + fn)
    alpha = jnp.exp2(m_old - m_new)
    p = jnp.exp2(t - (m_new - fn))
    l_ref[...] = alpha * l_ref[...] + jnp.sum(p, axis=-1, keepdims=True)
    acc_ref[...] = alpha * acc_ref[...] + _dot(p.astype(BF16), vcat_ref[...])
    m_ref[...] = m_new

    @pl.when(g == pl.num_programs(1) - 1)
    def _finish():
        o_ref[...] = (acc_ref[...] / l_ref[...]).astype(o_ref.dtype)


def _fox_sample(q, k, v, lf, cache_k, cache_v, layer, wt, page_table, nbd, t_new, nh, hd):
    page = cache_k.shape[2]
    assert nh == SUBLANES, "a cached (head, head_dim) slab must be exactly one f32 vreg tile"
    n_pages = page_table.shape[1]
    n_pg = _pick(n_pages, (16, 8, 4, 2, 1))
    rows = t_new * nh
    pw = page * nh

    assert wt.shape[0] % SUBLANES == 0

    def page_index(pt, b, g, j):
        return pt[b, n_pages - 1 - (g * n_pg + j)]

    def cache_spec(j):
        return pl.BlockSpec((None, None, page, nh, hd),
                            lambda b, g, pt: (layer, page_index(pt, b, g, j), 0, 0, 0))

    def suffix_spec(j):
        return pl.BlockSpec((SUBLANES, pw + LANES),
                            lambda b, g, pt: (lax.shift_right_logical(
                                page_index(pt, b, g, j), SUBLANES.bit_length() - 1), 0))

    new_spec = pl.BlockSpec((None, rows, hd), lambda b, g, pt: (b, 0, 0))
    in_specs = [new_spec, new_spec, new_spec,
                pl.BlockSpec((None, t_new, LANES), lambda b, g, pt: (b, 0, 0))]
    in_specs += [cache_spec(j) for j in range(n_pg)]
    in_specs += [cache_spec(j) for j in range(n_pg)]
    in_specs += [suffix_spec(j) for j in range(n_pg)]
    grid_spec = pltpu.PrefetchScalarGridSpec(
        num_scalar_prefetch=1,
        grid=(nbd, n_pages // n_pg),
        in_specs=in_specs,
        out_specs=new_spec,
        scratch_shapes=[pltpu.VMEM((rows, 1), F32),
                        pltpu.VMEM((rows, 1), F32),
                        pltpu.VMEM((rows, 1), F32),
                        pltpu.VMEM((rows, hd), F32),
                        pltpu.VMEM((1, LANES), F32),
                        pltpu.VMEM((n_pg * pw, hd), BF16),
                        pltpu.VMEM((n_pg * pw, hd), BF16)])
    to_rows = lambda a: a.reshape(nbd, rows, hd)
    out = pl.pallas_call(
        functools.partial(_fox_sample_kernel, n_pg=n_pg, nh=nh, page_slot=page_index),
        grid_spec=grid_spec,
        out_shape=jax.ShapeDtypeStruct((nbd, rows, hd), BF16),
        compiler_params=_params(2),
        name="fox_sample",
    )(page_table, to_rows(q), to_rows(k), to_rows(v), lf.reshape(nbd, t_new, LANES),
      *([cache_k] * n_pg), *([cache_v] * n_pg), *([wt] * n_pg))
    return out.reshape(nbd * t_new, nh * hd)


def _merge_kernel(ya_ref, o_ref, woa_ref, wob_ref, sa_ref, sb_ref, m_ref):
    a = _dot(ya_ref[...], woa_ref[...])
    b = _dot(o_ref[...], wob_ref[...])
    m_ref[...] = (sa_ref[...].astype(F32) * a + sb_ref[...].astype(F32) * b).astype(m_ref.dtype)


def _merge(ya, o, woa, wob, sa, sb):
    m, ca = ya.shape
    cb = o.shape[1]
    n = woa.shape[1]
    tm = _pick(m, (2048, 1024, 512, 256, 128))
    tn = _pick(n, (512, 256, 128))
    gspec = pl.BlockSpec((tm, tn), lambda i, j: (i, j))
    return pl.pallas_call(
        _merge_kernel,
        grid=(m // tm, n // tn),
        in_specs=[pl.BlockSpec((tm, ca), lambda i, j: (i, 0)),
                  pl.BlockSpec((tm, cb), lambda i, j: (i, 0)),
                  pl.BlockSpec((ca, tn), lambda i, j: (0, j)),
                  pl.BlockSpec((cb, tn), lambda i, j: (0, j)), gspec, gspec],
        out_specs=gspec,
        out_shape=jax.ShapeDtypeStruct((m, n), BF16),
        compiler_params=_params(2),
        name="mixer_merge",
    )(ya, o, woa, wob, sa, sb)


def _oproj_kernel(mix_ref, wo_ref, x_ref, g_ref, x1_ref, xn_ref):
    x1 = x_ref[...] + _dot(mix_ref[...], wo_ref[...])
    x1_ref[...] = x1
    xn_ref[...] = _rms_scale(x1, g_ref[...]).astype(xn_ref.dtype)


def _oproj(mix, wo, x, g):
    m, d = x.shape
    tm = _pick(m, (512, 256, 128))
    rspec = pl.BlockSpec((tm, d), lambda i: (i, 0))
    return pl.pallas_call(
        _oproj_kernel,
        grid=(m // tm,),
        in_specs=[rspec, pl.BlockSpec((d, d), lambda i: (0, 0)), rspec,
                  pl.BlockSpec((1, d), lambda i: (0, 0))],
        out_specs=[rspec, rspec],
        out_shape=[jax.ShapeDtypeStruct((m, d), F32), jax.ShapeDtypeStruct((m, d), BF16)],
        compiler_params=_params(1),
        name="oproj_residual_norm",
    )(mix, wo, x, g.reshape(1, d))


def _ffn_up_prompt_kernel(xn_ref, wu_ref, wg_ref, cw_ref, h_ref, tail_ref):
    xn = xn_ref[...]
    gpre = _dot(xn, wg_ref[...])
    tpos = lax.broadcasted_iota(jnp.int32, gpre.shape, 0)
    gc = _causal_conv(gpre, cw_ref[...], tpos, None)
    h_ref[...] = (gc * _sigmoid(gc) * _dot(xn, wu_ref[...])).astype(h_ref.dtype)
    nt = tail_ref.shape[0]
    tail_ref[...] = gpre[gpre.shape[0] - nt:, :]


def _ffn_up_sample_kernel(xn_ref, wu_ref, wg_ref, cw_ref, p1_ref, p2_ref, h_ref, g_ref, *, seq):
    xn = xn_ref[...]
    gpre = _dot(xn, wg_ref[...])
    tpos = lax.broadcasted_iota(jnp.int32, gpre.shape, 0) % seq
    gc = _causal_conv(gpre, cw_ref[...], tpos, [p1_ref[...], p2_ref[...]])
    h_ref[...] = (gc * _sigmoid(gc) * _dot(xn, wu_ref[...])).astype(h_ref.dtype)
    g_ref[...] = gpre


def _ffn_up_prompt(xn, wu, wg, cw, nb, seq):
    m, d = xn.shape
    f = wu.shape[1]
    km1 = cw.shape[0] - 1
    tn = _pick(f, (512, 256, 128))
    wspec = pl.BlockSpec((d, tn), lambda b, j: (0, j))
    return pl.pallas_call(
        _ffn_up_prompt_kernel,
        grid=(nb, f // tn),
        in_specs=[pl.BlockSpec((seq, d), lambda b, j: (b, 0)), wspec, wspec,
                  pl.BlockSpec((km1 + 1, tn), lambda b, j: (0, j))],
        out_specs=[pl.BlockSpec((seq, tn), lambda b, j: (b, j)),
                   pl.BlockSpec((None, km1, tn), lambda b, j: (b, 0, j))],
        out_shape=[jax.ShapeDtypeStruct((m, f), BF16), jax.ShapeDtypeStruct((nb, km1, f), F32)],
        compiler_params=_params(2),
        name="ffn_up_prompt",
    )(xn, wu, wg, cw)


def _ffn_up_sample(xn, wu, wg, cw, state, seq):
    m, d = xn.shape
    f = wu.shape[1]
    tn = _pick(f, (512, 256, 128))
    p1, p2 = _prev_rows(state, seq)
    wspec = pl.BlockSpec((d, tn), lambda j: (0, j))
    cspec = pl.BlockSpec((m, tn), lambda j: (0, j))
    h, gpre = pl.pallas_call(
        functools.partial(_ffn_up_sample_kernel, seq=seq),
        grid=(f // tn,),
        in_specs=[pl.BlockSpec((m, d), lambda j: (0, 0)), wspec, wspec,
                  pl.BlockSpec((cw.shape[0], tn), lambda j: (0, j)), cspec, cspec],
        out_specs=[cspec, cspec],
        out_shape=[jax.ShapeDtypeStruct((m, f), BF16), jax.ShapeDtypeStruct((m, f), F32)],
        compiler_params=_params(1),
        name="ffn_up_sample",
    )(xn, wu, wg, cw, p1, p2)
    return h, _new_state(state, gpre, seq)


def _ffn_down_kernel(h_ref, wd_ref, x1_ref, g_ref, y_ref, *, normalize):
    x2 = x1_ref[...] + _dot(h_ref[...], wd_ref[...])
    y_ref[...] = _rms_scale(x2, g_ref[...]) if normalize else x2


def _ffn_down(h, wd, x1, g, normalize):
    m, f = h.shape
    d = x1.shape[1]
    assert 2 * f * d <= VMEM_LIMIT // 2, "w_down must fit in VMEM next to the row tiles"
    tm = _pick(m, (512, 256, 128))
    rspec = pl.BlockSpec((tm, d), lambda i: (i, 0))
    return pl.pallas_call(
        functools.partial(_ffn_down_kernel, normalize=normalize),
        grid=(m // tm,),
        in_specs=[pl.BlockSpec((tm, f), lambda i: (i, 0)),
                  pl.BlockSpec((f, d), lambda i: (0, 0), pipeline_mode=pl.Buffered(1)),
                  rspec, pl.BlockSpec((1, d), lambda i: (0, 0))],
        out_specs=rspec,
        out_shape=jax.ShapeDtypeStruct((m, d), F32),
        compiler_params=_params(1),
        name="ffn_down_residual_norm",
    )(h, wd, x1, g.reshape(1, d))


def _in_proj_weights(w_in_l, b_f_l, d_conv, d_attn, nh, d_model):
    w = w_in_l.astype(BF16)
    sizes = (d_conv, d_conv, d_conv, d_attn, d_attn, d_attn, nh, d_model, d_model)
    cols, start = [], 0
    for s in sizes:
        cols.append(_Cols(w, start, s))
        start += s
    wf = cols[6]
    wf_pad = jnp.pad(w[:, wf.start:wf.start + nh], ((0, 0), (0, LANES - nh)))
    bf_pad = jnp.pad(b_f_l.astype(F32), (0, LANES - nh)).reshape(1, LANES)
    return cols[:6], cols[7:], wf_pad, bf_pad


def kernel(x_prompt, x_sample, cache_k, cache_v, cache_lf, state_conv_a, state_conv_ffn, page_table,
           norm_mix_g, w_in, b_f, conv_a_w, w_out_a, w_out_b, w_o, norm_ffn_g, w_up, w_gate,
           conv_ffn_w, w_down, norm_final_g):
    bp, tp, d_model = x_prompt.shape
    bd, td, _ = x_sample.shape
    depth = w_in.shape[0]
    nh, hd = cache_k.shape[3], cache_k.shape[4]
    d_attn = nh * hd
    d_conv = conv_a_w.shape[2]
    assert conv_a_w.shape[1] == 3 and conv_ffn_w.shape[1] == 3
    assert nh <= SUBLANES and hd % LANES == 0
    q_scale = hd ** -0.5 * LOG2E

    hp = x_prompt.reshape(bp * tp, d_model)
    hs = x_sample.reshape(bd * td, d_model)
    outs = [[] for _ in range(10)]
    for l in range(depth):
        (wh, wb, wc, wq, wk, wv), (wga, wgb), wf_pad, bf_pad = _in_proj_weights(
            w_in[l], b_f[l], d_conv, d_attn, nh, d_model)
        woa, wob, wo = w_out_a[l].astype(BF16), w_out_b[l].astype(BF16), w_o[l].astype(BF16)
        wu, wg, wd = w_up[l].astype(BF16), w_gate[l].astype(BF16), w_down[l].astype(BF16)
        last = l == depth - 1

        xn, sa, sb, lfp = _norm_gates(hp, norm_mix_g[l], wga, wgb, wf_pad, bf_pad)
        ya, cap = _mixa_prompt(xn, wh, wb, wc, conv_a_w[l], bp, tp)
        q, kp, vp = _qkv(xn, wq, wk, wv, q_scale)
        fcum = _cumsum_rows(lfp, bp, tp)
        o = _fox_prompt(q, kp, vp, fcum, bp, tp, nh, hd)
        mix = _merge(ya, o, woa, wob, sa, sb)
        x1, xn2 = _oproj(mix, wo, hp, norm_ffn_g[l])
        hact, cfp = _ffn_up_prompt(xn2, wu, wg, conv_ffn_w[l], bp, tp)
        hp = _ffn_down(hact, wd, x1, norm_final_g, last)

        xn, sa, sb, lfs = _norm_gates(hs, norm_mix_g[l], wga, wgb, wf_pad, bf_pad)
        ya, cas = _mixa_sample(xn, wh, wb, wc, conv_a_w[l], state_conv_a[l], td)
        q, ks, vs = _qkv(xn, wq, wk, wv, q_scale)
        wt = _page_suffix(cache_lf[l])
        o = _fox_sample(q, ks, vs, lfs, cache_k, cache_v, l, wt, page_table, bd, td, nh, hd)
        mix = _merge(ya, o, woa, wob, sa, sb)
        x1s, xn2 = _oproj(mix, wo, hs, norm_ffn_g[l])
        hact, cfs = _ffn_up_sample(xn2, wu, wg, conv_ffn_w[l], state_conv_ffn[l], td)
        hs = _ffn_down(hact, wd, x1s, norm_final_g, last)

        for lst, val in zip(outs, (kp.reshape(bp, tp, nh, hd), vp.reshape(bp, tp, nh, hd),
                                   lfp[:, :nh].reshape(bp, tp, nh), ks.reshape(bd, td, nh, hd),
                                   vs.reshape(bd, td, nh, hd), lfs[:, :nh].reshape(bd, td, nh),
                                   cap, cas, cfp, cfs)):
            lst.append(val)
    return (hp.reshape(bp, tp, d_model), hs.reshape(bd, td, d_model), *[jnp.stack(o_) for o_ in outs])
```

```python
import functools
from typing import NamedTuple

import jax
import jax.numpy as jnp
from jax import lax
from jax.experimental import pallas as pl
from jax.experimental.pallas import tpu as pltpu

F32 = jnp.float32
BF16 = jnp.bfloat16
EPS = 1e-6
LANES = 128
SUBLANES = 8
VMEM_LIMIT = 56 * 1024 * 1024
LOG2E = 1.4426950408889634


class _Cols(NamedTuple):
    array: jax.Array
    start: int
    width: int


def _col_operand(cols, tn):
    if cols.start % tn == 0:
        return cols.array, cols.start // tn
    return cols.array[:, cols.start:cols.start + cols.width], 0


def _pick(n, cands):
    for c in cands:
        if n % c == 0:
            return c
    return n


def _params(n_axes):
    return pltpu.CompilerParams(dimension_semantics=("arbitrary",) * n_axes,
                                vmem_limit_bytes=VMEM_LIMIT)


def _dot(a, b):
    return jnp.dot(a, b, preferred_element_type=F32)


def _dot_nt(a, b):
    return lax.dot_general(a, b, (((1,), (1,)), ((), ())), preferred_element_type=F32)


def _split3(x):
    hi = x.astype(BF16)
    r1 = x - hi.astype(F32)
    mid = r1.astype(BF16)
    lo = (r1 - mid.astype(F32)).astype(BF16)
    return hi, mid, lo


def _dot01(ones_mat, x, ones_on_left):
    out = None
    for part in _split3(x):
        t = _dot(ones_mat, part) if ones_on_left else _dot(part, ones_mat)
        out = t if out is None else out + t
    return out


def _sigmoid(x):
    return 1.0 / (1.0 + jnp.exp(-x))


def _log_sigmoid(x):
    return jnp.minimum(x, 0.0) - jnp.log1p(jnp.exp(-jnp.abs(x)))


def _rms_scale(x, g):
    var = jnp.mean(x * x, axis=-1, keepdims=True)
    return x * lax.rsqrt(var + EPS) * g


def _causal_conv(u, cw, tpos, prev):
    k = cw.shape[0]
    y = u * cw[k - 1:k, :]
    for j in range(1, k):
        shifted = pltpu.roll(u, j, axis=0)
        fill = 0.0 if prev is None else prev[j - 1]
        y = y + jnp.where(tpos >= j, shifted, fill) * cw[k - 1 - j:k - j, :]
    return y


def _conv_taps(src_ref, halo, r0, rows, taps):
    k = len(taps)
    y = src_ref[halo + r0:halo + r0 + rows, :] * taps[k - 1]
    for j in range(1, k):
        y = y + src_ref[halo + r0 - j:halo + r0 - j + rows, :] * taps[k - 1 - j]
    return y


def _mixa_prompt_kernel(xn_ref, wh_ref, wb_ref, wc_ref, cw_ref, ya_ref, tail_ref, u_scr, b_scr, *, chunk):
    xn = xn_ref[...]
    t, tn = b_scr.shape
    halo = u_scr.shape[0] - t
    k = cw_ref.shape[0]
    u_scr[0:halo, :] = jnp.zeros((halo, tn), F32)
    u_scr[halo:, :] = _dot(xn, wc_ref[...])
    b_scr[...] = _dot(xn, wh_ref[...])
    for r0 in range(0, t, chunk):
        rows = slice(halo + r0, halo + r0 + chunk)
        u_scr[rows, :] = u_scr[rows, :] * b_scr[r0:r0 + chunk, :]
    b_scr[...] = _dot(xn, wb_ref[...])
    cw = cw_ref[...]
    taps = [jnp.broadcast_to(cw[i:i + 1, :], (chunk, tn)) for i in range(k)]
    for r0 in range(0, t, chunk):
        y = _conv_taps(u_scr, halo, r0, chunk, taps)
        ya_ref[r0:r0 + chunk, :] = (b_scr[r0:r0 + chunk, :] * y).astype(ya_ref.dtype)
    tail_ref[...] = u_scr[halo + t - (k - 1):halo + t, :]


def _mixa_sample_kernel(xn_ref, wh_ref, wb_ref, wc_ref, cw_ref, p1_ref, p2_ref, ya_ref, u_ref, *, seq):
    xn = xn_ref[...]
    u = _dot(xn, wc_ref[...]) * _dot(xn, wh_ref[...])
    tpos = lax.broadcasted_iota(jnp.int32, u.shape, 0) % seq
    y = _causal_conv(u, cw_ref[...], tpos, [p1_ref[...], p2_ref[...]])
    ya_ref[...] = (_dot(xn, wb_ref[...]) * y).astype(ya_ref.dtype)
    u_ref[...] = u


def _prev_rows(state, seq):
    b, km1, c = state.shape
    outs = []
    for j in range(1, km1 + 1):
        rows = [state[:, km1 - j + t, :] if t < j else jnp.zeros((b, c), state.dtype) for t in range(seq)]
        outs.append(jnp.stack(rows, axis=1).reshape(b * seq, c))
    return outs


def _new_state(state, u, seq):
    b, km1, c = state.shape
    ext = jnp.concatenate([state, u.reshape(b, seq, c)], axis=1)
    return ext[:, -km1:, :]


def _mixa_prompt(xn, wh, wb, wc, cw, nb, seq):
    m, d = xn.shape
    c = wh.width
    km1 = cw.shape[0] - 1
    tn = _pick(c, (512, 256, 128))
    (wh, oh), (wb, ob), (wc, oc) = (_col_operand(w, tn) for w in (wh, wb, wc))
    wspec = lambda off: pl.BlockSpec((d, tn), lambda b, j: (0, j + off))
    assert km1 <= SUBLANES
    return pl.pallas_call(
        functools.partial(_mixa_prompt_kernel, chunk=_pick(seq, (64, 32, 16, 8))),
        grid=(nb, c // tn),
        in_specs=[pl.BlockSpec((seq, d), lambda b, j: (b, 0)), wspec(oh), wspec(ob), wspec(oc),
                  pl.BlockSpec((km1 + 1, tn), lambda b, j: (0, j))],
        out_specs=[pl.BlockSpec((seq, tn), lambda b, j: (b, j)),
                   pl.BlockSpec((None, km1, tn), lambda b, j: (b, 0, j))],
        out_shape=[jax.ShapeDtypeStruct((m, c), BF16),
                   jax.ShapeDtypeStruct((nb, km1, c), F32)],
        scratch_shapes=[pltpu.VMEM((seq + SUBLANES, tn), F32), pltpu.VMEM((seq, tn), F32)],
        compiler_params=_params(2),
        name="mixa_prompt",
    )(xn, wh, wb, wc, cw)


def _mixa_sample(xn, wh, wb, wc, cw, state, seq):
    m, d = xn.shape
    c = wh.width
    tn = _pick(c, (512, 256, 128))
    p1, p2 = _prev_rows(state, seq)
    (wh, oh), (wb, ob), (wc, oc) = (_col_operand(w, tn) for w in (wh, wb, wc))
    wspec = lambda off: pl.BlockSpec((d, tn), lambda j: (0, j + off))
    cspec = pl.BlockSpec((m, tn), lambda j: (0, j))
    ya, u = pl.pallas_call(
        functools.partial(_mixa_sample_kernel, seq=seq),
        grid=(c // tn,),
        in_specs=[pl.BlockSpec((m, d), lambda j: (0, 0)), wspec(oh), wspec(ob), wspec(oc),
                  pl.BlockSpec((cw.shape[0], tn), lambda j: (0, j)), cspec, cspec],
        out_specs=[cspec, cspec],
        out_shape=[jax.ShapeDtypeStruct((m, c), BF16), jax.ShapeDtypeStruct((m, c), F32)],
        compiler_params=_params(1),
        name="mixa_sample",
    )(xn, wh, wb, wc, cw, p1, p2)
    return ya, _new_state(state, u, seq)


def _qkv_kernel(xn_ref, wq_ref, wk_ref, wv_ref, q_ref, k_ref, v_ref, *, q_scale):
    xn = xn_ref[...]
    q_ref[...] = (_dot(xn, wq_ref[...]) * q_scale).astype(q_ref.dtype)
    k_ref[...] = _dot(xn, wk_ref[...])
    v_ref[...] = _dot(xn, wv_ref[...])


def _qkv(xn, wq, wk, wv, q_scale):
    m, d = xn.shape
    n = wq.width
    tm = _pick(m, (1024, 512, 256, 128))
    tn = _pick(n, (512, 256, 128))
    (wq, oq), (wk, ok), (wv, ov) = (_col_operand(w, tn) for w in (wq, wk, wv))
    wspec = lambda off: pl.BlockSpec((d, tn), lambda i, j: (0, j + off))
    ospec = pl.BlockSpec((tm, tn), lambda i, j: (i, j))
    return pl.pallas_call(
        functools.partial(_qkv_kernel, q_scale=q_scale),
        grid=(m // tm, n // tn),
        in_specs=[pl.BlockSpec((tm, d), lambda i, j: (i, 0)), wspec(oq), wspec(ok), wspec(ov)],
        out_specs=[ospec, ospec, ospec],
        out_shape=[jax.ShapeDtypeStruct((m, n), BF16), jax.ShapeDtypeStruct((m, n), F32),
                   jax.ShapeDtypeStruct((m, n), F32)],
        compiler_params=_params(2),
        name="qkv_proj",
    )(xn, wq, wk, wv)


def _gates_kernel(x_ref, g_ref, wa_ref, wb_ref, wf_ref, bf_ref, xn_ref, sa_ref, sb_ref, lf_ref):
    xn = _rms_scale(x_ref[...], g_ref[...]).astype(xn_ref.dtype)
    xn_ref[...] = xn
    lf_ref[...] = _log_sigmoid(_dot(xn, wf_ref[...]) + bf_ref[...])
    sa_ref[...] = _sigmoid(_dot(xn, wa_ref[...])).astype(sa_ref.dtype)
    sb_ref[...] = _sigmoid(_dot(xn, wb_ref[...])).astype(sb_ref.dtype)


def _norm_gates(x, g, wa, wb, wf_pad, bf_pad):
    m, d = x.shape
    n = wa.width
    tm = _pick(m, (512, 256, 128))
    wa, wb = (w.array[:, w.start:w.start + w.width] for w in (wa, wb))
    assert 2 * (2 * d * n) <= VMEM_LIMIT // 2
    once = dict(pipeline_mode=pl.Buffered(1))
    rspec = pl.BlockSpec((tm, d), lambda i: (i, 0))
    ospec = pl.BlockSpec((tm, n), lambda i: (i, 0))
    return pl.pallas_call(
        _gates_kernel,
        grid=(m // tm,),
        in_specs=[rspec, pl.BlockSpec((1, d), lambda i: (0, 0)),
                  pl.BlockSpec((d, n), lambda i: (0, 0), **once),
                  pl.BlockSpec((d, n), lambda i: (0, 0), **once),
                  pl.BlockSpec((d, LANES), lambda i: (0, 0)),
                  pl.BlockSpec((1, LANES), lambda i: (0, 0))],
        out_specs=[rspec, ospec, ospec, pl.BlockSpec((tm, LANES), lambda i: (i, 0))],
        out_shape=[jax.ShapeDtypeStruct((m, d), BF16), jax.ShapeDtypeStruct((m, n), BF16),
                   jax.ShapeDtypeStruct((m, n), BF16), jax.ShapeDtypeStruct((m, LANES), F32)],
        compiler_params=_params(1),
        name="norm_gate_proj",
    )(x, g.reshape(1, d), wa, wb, wf_pad, bf_pad)


def _cumsum_kernel(lf_ref, f_ref, *, blk):
    t = lf_ref.shape[0]
    r = lax.broadcasted_iota(jnp.int32, (blk, blk), 0)
    c = lax.broadcasted_iota(jnp.int32, (blk, blk), 1)
    lower = (c <= r).astype(BF16)
    carry = jnp.zeros((1, lf_ref.shape[1]), F32)
    for i in range(t // blk):
        y = _dot01(lower, lf_ref[i * blk:(i + 1) * blk, :], True) + carry
        f_ref[i * blk:(i + 1) * blk, :] = y
        carry = y[blk - 1:blk, :]


def _cumsum_rows(lf, nb, seq):
    blk = _pick(seq, (256, 128))
    return pl.pallas_call(
        functools.partial(_cumsum_kernel, blk=blk),
        grid=(nb,),
        in_specs=[pl.BlockSpec((seq, LANES), lambda b: (b, 0))],
        out_specs=pl.BlockSpec((seq, LANES), lambda b: (b, 0)),
        out_shape=jax.ShapeDtypeStruct(lf.shape, F32),
        compiler_params=_params(1),
        name="logforget_cumsum",
    )(lf)


def _fox_prompt_kernel(q_ref, k_ref, v_ref, fcol_ref, frow_ref, o_ref, *, hg, hd, nq):
    hgrp = pl.program_id(1)
    i = pl.program_id(2)
    tq = q_ref.shape[0]
    lane = lax.broadcasted_iota(jnp.int32, fcol_ref.shape, 1)
    cols_of = lambda e: slice(e * hd, (e + 1) * hd)

    def attend(c):
        past = c * tq
        rows = lax.broadcasted_iota(jnp.int32, (tq, tq), 0)
        cols = lax.broadcasted_iota(jnp.int32, (tq, tq), 1)
        for e in range(hg):
            q = q_ref[:, cols_of(e)]
            fq = jnp.sum(jnp.where(lane == hgrp * hg + e, fcol_ref[...], 0.0),
                         axis=-1, keepdims=True) * LOG2E
            kd = k_ref[past:past + tq, cols_of(e)].astype(BF16)
            td = _dot_nt(q, kd) - frow_ref[e:e + 1, past:past + tq] * LOG2E
            td = jnp.where(cols <= rows, td, -jnp.inf)
            mx = jnp.max(td, axis=-1, keepdims=True)
            if c:
                kp = k_ref[0:past, cols_of(e)].astype(BF16)
                tp = _dot_nt(q, kp) - frow_ref[e:e + 1, 0:past] * LOG2E
                mx = jnp.maximum(mx, jnp.max(tp, axis=-1, keepdims=True))
            shift = (mx + fq) - fq
            pd = jnp.exp2(td - shift)
            l = jnp.sum(pd, axis=-1, keepdims=True)
            acc = _dot(pd.astype(BF16), v_ref[past:past + tq, cols_of(e)].astype(BF16))
            if c:
                pp = jnp.exp2(tp - shift)
                l = l + jnp.sum(pp, axis=-1, keepdims=True)
                acc = acc + _dot(pp.astype(BF16), v_ref[0:past, cols_of(e)].astype(BF16))
            o_ref[:, cols_of(e)] = (acc / l).astype(o_ref.dtype)

    for c in range(nq):
        pl.when(i == c)(functools.partial(attend, c))


def _fox_prompt(q, k, v, fcum, nb, seq, nh, hd):
    m = q.shape[0]
    tq = _pick(seq, (512, 256, 128))
    nq = seq // tq
    hg = _pick(nh, (4, 2, 1))
    ng = nh // hg
    frow = fcum[:, :nh].reshape(nb, seq, nh).transpose(0, 2, 1).reshape(nb * ng, hg, seq)
    return pl.pallas_call(
        functools.partial(_fox_prompt_kernel, hg=hg, hd=hd, nq=nq),
        grid=(nb, ng, nq),
        in_specs=[pl.BlockSpec((tq, hg * hd), lambda b, h, i: (b * nq + i, h)),
                  pl.BlockSpec((seq, hg * hd), lambda b, h, i: (b, h)),
                  pl.BlockSpec((seq, hg * hd), lambda b, h, i: (b, h)),
                  pl.BlockSpec((tq, LANES), lambda b, h, i: (b * nq + i, 0)),
                  pl.BlockSpec((None, hg, seq), lambda b, h, i: (b * ng + h, 0, 0))],
        out_specs=pl.BlockSpec((tq, hg * hd), lambda b, h, i: (b * nq + i, h)),
        out_shape=jax.ShapeDtypeStruct((m, nh * hd), BF16),
        compiler_params=_params(3),
        name="fox_prompt",
    )(q, k, v, fcum, frow)


def _suffix_kernel(lf_ref, o_ref, *, nh):
    w = lf_ref.shape[1]
    wo = o_ref.shape[1]
    shift = nh.bit_length() - 1
    r = lax.broadcasted_iota(jnp.int32, (w, wo), 0)
    c = lax.broadcasted_iota(jnp.int32, (w, wo), 1)
    same_head = jnp.bitwise_and(r, nh - 1) == jnp.bitwise_and(c, nh - 1)
    later_key = lax.shift_right_logical(r, shift) > lax.shift_right_logical(c, shift)
    sel = jnp.logical_and(same_head, jnp.logical_or(later_key, c >= w)).astype(BF16)
    o_ref[...] = _dot01(sel, lf_ref[...], False)


def _page_suffix(cache_lf_l):
    n_pool, page, nh = cache_lf_l.shape
    assert nh & (nh - 1) == 0 and LANES % nh == 0
    w = page * nh
    pp = _pick(n_pool, (512, 256, 128, 64, 32, 16, 8))
    out = pl.pallas_call(
        functools.partial(_suffix_kernel, nh=nh),
        grid=(n_pool // pp,),
        in_specs=[pl.BlockSpec((pp, w), lambda i: (i, 0))],
        out_specs=pl.BlockSpec((pp, w + LANES), lambda i: (i, 0)),
        out_shape=jax.ShapeDtypeStruct((n_pool, w + LANES), F32),
        compiler_params=_params(1),
        name="page_suffix",
    )(cache_lf_l.reshape(n_pool, w))
    return out


def _fox_sample_kernel(pt_ref, q_ref, kn_ref, vn_ref, lfn_ref, *rest, n_pg, nh, page_slot):
    k_refs = rest[:n_pg]
    v_refs = rest[n_pg:2 * n_pg]
    w_refs = rest[2 * n_pg:3 * n_pg]
    o_ref = rest[3 * n_pg]
    fn_ref, m_ref, l_ref, acc_ref, tail_ref, kcat_ref, vcat_ref = rest[3 * n_pg + 1:]
    g = pl.program_id(1)
    rows, hd = q_ref.shape
    t_new = rows // nh
    pw = k_refs[0].shape[0] * nh

    @pl.when(g == 0)
    def _init():
        eye = (lax.broadcasted_iota(jnp.int32, (nh, LANES), 0)
               == lax.broadcasted_iota(jnp.int32, (nh, LANES), 1))
        qf = q_ref[...].astype(F32)
        kn = kn_ref[...]
        vn = vn_ref[...]
        lfn = lfn_ref[...]
        fcol = []
        run = jnp.zeros((1, LANES), F32)
        for t in range(t_new):
            run = run + lfn[t:t + 1, :]
            fcol.append(jnp.sum(jnp.where(eye, jnp.broadcast_to(run, (nh, LANES)), 0.0),
                                axis=-1, keepdims=True) * LOG2E)
        for t in range(t_new):
            rs = slice(t * nh, (t + 1) * nh)
            s = [jnp.sum(qf[rs, :] * kn[j * nh:(j + 1) * nh, :], axis=-1, keepdims=True)
                 + (fcol[t] - fcol[j]) for j in range(t + 1)]
            m = s[0]
            for sj in s[1:]:
                m = jnp.maximum(m, sj)
            l = jnp.zeros((nh, 1), F32)
            acc = jnp.zeros((nh, hd), F32)
            for j, sj in enumerate(s):
                p = jnp.exp2(sj - m)
                l = l + p
                acc = acc + p * vn[j * nh:(j + 1) * nh, :]
            fn_ref[rs, :] = fcol[t]
            m_ref[rs, :] = m
            l_ref[rs, :] = l
            acc_ref[rs, :] = acc
        tail_ref[...] = jnp.zeros_like(tail_ref)

    reps = pw // LANES
    tail = tail_ref[...]
    bias = []
    for j in range(n_pg):
        kcat_ref[j * pw:(j + 1) * pw, :] = k_refs[j][...].reshape(pw, hd).astype(kcat_ref.dtype)
        vcat_ref[j * pw:(j + 1) * pw, :] = v_refs[j][...].reshape(pw, hd).astype(vcat_ref.dtype)
        slot = jnp.bitwise_and(page_slot(pt_ref, pl.program_id(0), g, j), SUBLANES - 1)
        w = w_refs[j][pl.ds(slot, 1), :]
        bias.append(w[:, :pw] + jnp.concatenate([tail] * reps, axis=1))
        tail = tail + w[:, pw:]
    tail_ref[...] = tail
    lane = lax.broadcasted_iota(jnp.int32, (rows, LANES), 1)
    row = lax.broadcasted_iota(jnp.int32, (rows, LANES), 0)
    own = jnp.bitwise_and(lane, nh - 1) == jnp.bitwise_and(row, nh - 1)
    mask = jnp.where(own, 0.0, -jnp.inf)
    bias = jnp.concatenate(bias, axis=1) * LOG2E
    t = _dot_nt(q_ref[...], kcat_ref[...])
    t = t + (bias + jnp.concatenate([mask] * (n_pg * reps), axis=1))
    fn = fn_ref[...]
    m_old = m_ref[...]
    m_new = jnp.maximum(m_old, jnp.max(t, axis=-1, keepdims=True) + fn)
    alpha = jnp.exp2(m_old - m_new)
    p = jnp.exp2(t - (m_new - fn))
    l_ref[...] = alpha * l_ref[...] + jnp.sum(p, axis=-1, keepdims=True)
    acc_ref[...] = alpha * acc_ref[...] + _dot(p.astype(BF16), vcat_ref[...])
    m_ref[...] = m_new

    @pl.when(g == pl.num_programs(1) - 1)
    def _finish():
        o_ref[...] = (acc_ref[...] / l_ref[...]).astype(o_ref.dtype)


def _fox_sample(q, k, v, lf, cache_k, cache_v, layer, wt, page_table, nbd, t_new, nh, hd):
    page = cache_k.shape[2]
    assert nh == SUBLANES, "a cached (head, head_dim) slab must be exactly one f32 vreg tile"
    n_pages = page_table.shape[1]
    n_pg = _pick(n_pages, (16, 8, 4, 2, 1))
    rows = t_new * nh
    pw = page * nh

    assert wt.shape[0] % SUBLANES == 0

    def page_index(pt, b, g, j):
        return pt[b, n_pages - 1 - (g * n_pg + j)]

    def cache_spec(j):
        return pl.BlockSpec((None, None, page, nh, hd),
                            lambda b, g, pt: (layer, page_index(pt, b, g, j), 0, 0, 0))

    def suffix_spec(j):
        return pl.BlockSpec((SUBLANES, pw + LANES),
                            lambda b, g, pt: (lax.shift_right_logical(
                                page_index(pt, b, g, j), SUBLANES.bit_length() - 1), 0))

    new_spec = pl.BlockSpec((None, rows, hd), lambda b, g, pt: (b, 0, 0))
    in_specs = [new_spec, new_spec, new_spec,
                pl.BlockSpec((None, t_new, LANES), lambda b, g, pt: (b, 0, 0))]
    in_specs += [cache_spec(j) for j in range(n_pg)]
    in_specs += [cache_spec(j) for j in range(n_pg)]
    in_specs += [suffix_spec(j) for j in range(n_pg)]
    grid_spec = pltpu.PrefetchScalarGridSpec(
        num_scalar_prefetch=1,
        grid=(nbd, n_pages // n_pg),
        in_specs=in_specs,
        out_specs=new_spec,
        scratch_shapes=[pltpu.VMEM((rows, 1), F32),
                        pltpu.VMEM((rows, 1), F32),
                        pltpu.VMEM((rows, 1), F32),
                        pltpu.VMEM((rows, hd), F32),
                        pltpu.VMEM((1, LANES), F32),
                        pltpu.VMEM((n_pg * pw, hd), BF16),
                        pltpu.VMEM((n_pg * pw, hd), BF16)])
    to_rows = lambda a: a.reshape(nbd, rows, hd)
    out = pl.pallas_call(
        functools.partial(_fox_sample_kernel, n_pg=n_pg, nh=nh, page_slot=page_index),
        grid_spec=grid_spec,
        out_shape=jax.ShapeDtypeStruct((nbd, rows, hd), BF16),
        compiler_params=_params(2),
        name="fox_sample",
    )(page_table, to_rows(q), to_rows(k), to_rows(v), lf.reshape(nbd, t_new, LANES),
      *([cache_k] * n_pg), *([cache_v] * n_pg), *([wt] * n_pg))
    return out.reshape(nbd * t_new, nh * hd)


def _merge_kernel(ya_ref, o_ref, woa_ref, wob_ref, sa_ref, sb_ref, m_ref):
    a = _dot(ya_ref[...], woa_ref[...])
    b = _dot(o_ref[...], wob_ref[...])
    m_ref[...] = (sa_ref[...].astype(F32) * a + sb_ref[...].astype(F32) * b).astype(m_ref.dtype)


def _merge(ya, o, woa, wob, sa, sb):
    m, ca = ya.shape
    cb = o.shape[1]
    n = woa.shape[1]
    tm = _pick(m, (2048, 1024, 512, 256, 128))
    tn = _pick(n, (512, 256, 128))
    gspec = pl.BlockSpec((tm, tn), lambda i, j: (i, j))
    return pl.pallas_call(
        _merge_kernel,
        grid=(m // tm, n // tn),
        in_specs=[pl.BlockSpec((tm, ca), lambda i, j: (i, 0)),
                  pl.BlockSpec((tm, cb), lambda i, j: (i, 0)),
                  pl.BlockSpec((ca, tn), lambda i, j: (0, j)),
                  pl.BlockSpec((cb, tn), lambda i, j: (0, j)), gspec, gspec],
        out_specs=gspec,
        out_shape=jax.ShapeDtypeStruct((m, n), BF16),
        compiler_params=_params(2),
        name="mixer_merge",
    )(ya, o, woa, wob, sa, sb)


def _oproj_kernel(mix_ref, wo_ref, x_ref, g_ref, x1_ref, xn_ref):
    x1 = x_ref[...] + _dot(mix_ref[...], wo_ref[...])
    x1_ref[...] = x1
    xn_ref[...] = _rms_scale(x1, g_ref[...]).astype(xn_ref.dtype)


def _oproj(mix, wo, x, g):
    m, d = x.shape
    tm = _pick(m, (512, 256, 128))
    rspec = pl.BlockSpec((tm, d), lambda i: (i, 0))
    return pl.pallas_call(
        _oproj_kernel,
        grid=(m // tm,),
        in_specs=[rspec, pl.BlockSpec((d, d), lambda i: (0, 0)), rspec,
                  pl.BlockSpec((1, d), lambda i: (0, 0))],
        out_specs=[rspec, rspec],
        out_shape=[jax.ShapeDtypeStruct((m, d), F32), jax.ShapeDtypeStruct((m, d), BF16)],
        compiler_params=_params(1),
        name="oproj_residual_norm",
    )(mix, wo, x, g.reshape(1, d))


def _ffn_up_prompt_kernel(xn_ref, wu_ref, wg_ref, cw_ref, h_ref, tail_ref, g_scr, u_scr, *, chunk):
    xn = xn_ref[...]
    t, tn = u_scr.shape
    halo = g_scr.shape[0] - t
    k = cw_ref.shape[0]
    g_scr[0:halo, :] = jnp.zeros((halo, tn), F32)
    g_scr[halo:, :] = _dot(xn, wg_ref[...])
    u_scr[...] = _dot(xn, wu_ref[...])
    cw = cw_ref[...]
    taps = [jnp.broadcast_to(cw[i:i + 1, :], (chunk, tn)) for i in range(k)]
    for r0 in range(0, t, chunk):
        gc = _conv_taps(g_scr, halo, r0, chunk, taps)
        h_ref[r0:r0 + chunk, :] = (gc * _sigmoid(gc) * u_scr[r0:r0 + chunk, :]).astype(h_ref.dtype)
    tail_ref[...] = g_scr[halo + t - (k - 1):halo + t, :]


def _ffn_up_sample_kernel(xn_ref, wu_ref, wg_ref, cw_ref, p1_ref, p2_ref, h_ref, g_ref, *, seq):
    xn = xn_ref[...]
    gpre = _dot(xn, wg_ref[...])
    tpos = lax.broadcasted_iota(jnp.int32, gpre.shape, 0) % seq
    gc = _causal_conv(gpre, cw_ref[...], tpos, [p1_ref[...], p2_ref[...]])
    h_ref[...] = (gc * _sigmoid(gc) * _dot(xn, wu_ref[...])).astype(h_ref.dtype)
    g_ref[...] = gpre


def _ffn_up_prompt(xn, wu, wg, cw, nb, seq):
    m, d = xn.shape
    f = wu.shape[1]
    km1 = cw.shape[0] - 1
    tn = _pick(f, (512, 256, 128))
    wspec = pl.BlockSpec((d, tn), lambda b, j: (0, j))
    assert km1 <= SUBLANES
    return pl.pallas_call(
        functools.partial(_ffn_up_prompt_kernel, chunk=_pick(seq, (64, 32, 16, 8))),
        grid=(nb, f // tn),
        in_specs=[pl.BlockSpec((seq, d), lambda b, j: (b, 0)), wspec, wspec,
                  pl.BlockSpec((km1 + 1, tn), lambda b, j: (0, j))],
        out_specs=[pl.BlockSpec((seq, tn), lambda b, j: (b, j)),
                   pl.BlockSpec((None, km1, tn), lambda b, j: (b, 0, j))],
        out_shape=[jax.ShapeDtypeStruct((m, f), BF16), jax.ShapeDtypeStruct((nb, km1, f), F32)],
        scratch_shapes=[pltpu.VMEM((seq + SUBLANES, tn), F32), pltpu.VMEM((seq, tn), F32)],
        compiler_params=_params(2),
        name="ffn_up_prompt",
    )(xn, wu, wg, cw)


def _ffn_up_sample(xn, wu, wg, cw, state, seq):
    m, d = xn.shape
    f = wu.shape[1]
    tn = _pick(f, (512, 256, 128))
    p1, p2 = _prev_rows(state, seq)
    wspec = pl.BlockSpec((d, tn), lambda j: (0, j))
    cspec = pl.BlockSpec((m, tn), lambda j: (0, j))
    h, gpre = pl.pallas_call(
        functools.partial(_ffn_up_sample_kernel, seq=seq),
        grid=(f // tn,),
        in_specs=[pl.BlockSpec((m, d), lambda j: (0, 0)), wspec, wspec,
                  pl.BlockSpec((cw.shape[0], tn), lambda j: (0, j)), cspec, cspec],
        out_specs=[cspec, cspec],
        out_shape=[jax.ShapeDtypeStruct((m, f), BF16), jax.ShapeDtypeStruct((m, f), F32)],
        compiler_params=_params(1),
        name="ffn_up_sample",
    )(xn, wu, wg, cw, p1, p2)
    return h, _new_state(state, gpre, seq)


def _ffn_down_kernel(h_ref, wd_ref, x1_ref, g_ref, y_ref, *, normalize):
    x2 = x1_ref[...] + _dot(h_ref[...], wd_ref[...])
    y_ref[...] = _rms_scale(x2, g_ref[...]) if normalize else x2


def _ffn_down(h, wd, x1, g, normalize):
    m, f = h.shape
    d = x1.shape[1]
    assert 2 * f * d <= VMEM_LIMIT // 2, "w_down must fit in VMEM next to the row tiles"
    tm = _pick(m, (512, 256, 128))
    rspec = pl.BlockSpec((tm, d), lambda i: (i, 0))
    return pl.pallas_call(
        functools.partial(_ffn_down_kernel, normalize=normalize),
        grid=(m // tm,),
        in_specs=[pl.BlockSpec((tm, f), lambda i: (i, 0)),
                  pl.BlockSpec((f, d), lambda i: (0, 0), pipeline_mode=pl.Buffered(1)),
                  rspec, pl.BlockSpec((1, d), lambda i: (0, 0))],
        out_specs=rspec,
        out_shape=jax.ShapeDtypeStruct((m, d), F32),
        compiler_params=_params(1),
        name="ffn_down_residual_norm",
    )(h, wd, x1, g.reshape(1, d))


def _in_proj_weights(w_in_l, b_f_l, d_conv, d_attn, nh, d_model):
    w = w_in_l.astype(BF16)
    sizes = (d_conv, d_conv, d_conv, d_attn, d_attn, d_attn, nh, d_model, d_model)
    cols, start = [], 0
    for s in sizes:
        cols.append(_Cols(w, start, s))
        start += s
    wf = cols[6]
    wf_pad = jnp.pad(w[:, wf.start:wf.start + nh], ((0, 0), (0, LANES - nh)))
    bf_pad = jnp.pad(b_f_l.astype(F32), (0, LANES - nh)).reshape(1, LANES)
    return cols[:6], cols[7:], wf_pad, bf_pad


def kernel(x_prompt, x_sample, cache_k, cache_v, cache_lf, state_conv_a, state_conv_ffn, page_table,
           norm_mix_g, w_in, b_f, conv_a_w, w_out_a, w_out_b, w_o, norm_ffn_g, w_up, w_gate,
           conv_ffn_w, w_down, norm_final_g):
    bp, tp, d_model = x_prompt.shape
    bd, td, _ = x_sample.shape
    depth = w_in.shape[0]
    nh, hd = cache_k.shape[3], cache_k.shape[4]
    d_attn = nh * hd
    d_conv = conv_a_w.shape[2]
    assert conv_a_w.shape[1] == 3 and conv_ffn_w.shape[1] == 3
    assert nh <= SUBLANES and hd % LANES == 0
    q_scale = hd ** -0.5 * LOG2E

    hp = x_prompt.reshape(bp * tp, d_model)
    hs = x_sample.reshape(bd * td, d_model)
    outs = [[] for _ in range(10)]
    for l in range(depth):
        (wh, wb, wc, wq, wk, wv), (wga, wgb), wf_pad, bf_pad = _in_proj_weights(
            w_in[l], b_f[l], d_conv, d_attn, nh, d_model)
        woa, wob, wo = w_out_a[l].astype(BF16), w_out_b[l].astype(BF16), w_o[l].astype(BF16)
        wu, wg, wd = w_up[l].astype(BF16), w_gate[l].astype(BF16), w_down[l].astype(BF16)
        last = l == depth - 1

        xn, sa, sb, lfp = _norm_gates(hp, norm_mix_g[l], wga, wgb, wf_pad, bf_pad)
        ya, cap = _mixa_prompt(xn, wh, wb, wc, conv_a_w[l], bp, tp)
        q, kp, vp = _qkv(xn, wq, wk, wv, q_scale)
        fcum = _cumsum_rows(lfp, bp, tp)
        o = _fox_prompt(q, kp, vp, fcum, bp, tp, nh, hd)
        mix = _merge(ya, o, woa, wob, sa, sb)
        x1, xn2 = _oproj(mix, wo, hp, norm_ffn_g[l])
        hact, cfp = _ffn_up_prompt(xn2, wu, wg, conv_ffn_w[l], bp, tp)
        hp = _ffn_down(hact, wd, x1, norm_final_g, last)

        xn, sa, sb, lfs = _norm_gates(hs, norm_mix_g[l], wga, wgb, wf_pad, bf_pad)
        ya, cas = _mixa_sample(xn, wh, wb, wc, conv_a_w[l], state_conv_a[l], td)
        q, ks, vs = _qkv(xn, wq, wk, wv, q_scale)
        wt = _page_suffix(cache_lf[l])
        o = _fox_sample(q, ks, vs, lfs, cache_k, cache_v, l, wt, page_table, bd, td, nh, hd)
        mix = _merge(ya, o, woa, wob, sa, sb)
        x1s, xn2 = _oproj(mix, wo, hs, norm_ffn_g[l])
        hact, cfs = _ffn_up_sample(xn2, wu, wg, conv_ffn_w[l], state_conv_ffn[l], td)
        hs = _ffn_down(hact, wd, x1s, norm_final_g, last)

        for lst, val in zip(outs, (kp.reshape(bp, tp, nh, hd), vp.reshape(bp, tp, nh, hd),
                                   lfp[:, :nh].reshape(bp, tp, nh), ks.reshape(bd, td, nh, hd),
                                   vs.reshape(bd, td, nh, hd), lfs[:, :nh].reshape(bd, td, nh),
                                   cap, cas, cfp, cfs)):
            lst.append(val)
    return (hp.reshape(bp, tp, d_model), hs.reshape(bd, td, d_model), *[jnp.stack(o_) for o_ in outs])
```

```python
import functools
from typing import NamedTuple

import jax
import jax.numpy as jnp
from jax import lax
from jax.experimental import pallas as pl
from jax.experimental.pallas import tpu as pltpu

F32 = jnp.float32
BF16 = jnp.bfloat16
EPS = 1e-6
LANES = 128
SUBLANES = 8
VMEM_LIMIT = 56 * 1024 * 1024
LOG2E = 1.4426950408889634


class _Cols(NamedTuple):
    array: jax.Array
    start: int
    width: int


def _col_operand(cols, tn):
    if cols.start % tn == 0:
        return cols.array, cols.start // tn
    return cols.array[:, cols.start:cols.start + cols.width], 0


def _pick(n, cands):
    for c in cands:
        if n % c == 0:
            return c
    return n


def _params(n_axes):
    return pltpu.CompilerParams(dimension_semantics=("arbitrary",) * n_axes,
                                vmem_limit_bytes=VMEM_LIMIT)


def _dot(a, b):
    return jnp.dot(a, b, preferred_element_type=F32)


def _dot_nt(a, b):
    return lax.dot_general(a, b, (((1,), (1,)), ((), ())), preferred_element_type=F32)


def _split3(x):
    hi = x.astype(BF16)
    r1 = x - hi.astype(F32)
    mid = r1.astype(BF16)
    lo = (r1 - mid.astype(F32)).astype(BF16)
    return hi, mid, lo


def _dot01(ones_mat, x, ones_on_left):
    out = None
    for part in _split3(x):
        t = _dot(ones_mat, part) if ones_on_left else _dot(part, ones_mat)
        out = t if out is None else out + t
    return out


def _sigmoid(x):
    return 1.0 / (1.0 + jnp.exp(-x))


def _log_sigmoid(x):
    return jnp.minimum(x, 0.0) - jnp.log1p(jnp.exp(-jnp.abs(x)))


def _rms_scale(x, g):
    var = jnp.mean(x * x, axis=-1, keepdims=True)
    return x * lax.rsqrt(var + EPS) * g


def _causal_conv(u, cw, tpos, prev):
    k = cw.shape[0]
    y = u * cw[k - 1:k, :]
    for j in range(1, k):
        shifted = pltpu.roll(u, j, axis=0)
        fill = 0.0 if prev is None else prev[j - 1]
        y = y + jnp.where(tpos >= j, shifted, fill) * cw[k - 1 - j:k - j, :]
    return y


def _conv_taps(src_ref, halo, r0, rows, taps):
    k = len(taps)
    y = src_ref[halo + r0:halo + r0 + rows, :] * taps[k - 1]
    for j in range(1, k):
        y = y + src_ref[halo + r0 - j:halo + r0 - j + rows, :] * taps[k - 1 - j]
    return y


def _mixa_prompt_kernel(xn_ref, wh_ref, wb_ref, wc_ref, cw_ref, ya_ref, tail_ref, u_scr, b_scr, *, chunk):
    xn = xn_ref[...]
    t, tn = b_scr.shape
    halo = u_scr.shape[0] - t
    k = cw_ref.shape[0]
    u_scr[0:halo, :] = jnp.zeros((halo, tn), F32)
    u_scr[halo:, :] = _dot(xn, wc_ref[...])
    b_scr[...] = _dot(xn, wh_ref[...])
    for r0 in range(0, t, chunk):
        rows = slice(halo + r0, halo + r0 + chunk)
        u_scr[rows, :] = u_scr[rows, :] * b_scr[r0:r0 + chunk, :]
    b_scr[...] = _dot(xn, wb_ref[...])
    cw = cw_ref[...]
    taps = [jnp.broadcast_to(cw[i:i + 1, :], (chunk, tn)) for i in range(k)]
    for r0 in range(0, t, chunk):
        y = _conv_taps(u_scr, halo, r0, chunk, taps)
        ya_ref[r0:r0 + chunk, :] = (b_scr[r0:r0 + chunk, :] * y).astype(ya_ref.dtype)
    tail_ref[...] = u_scr[halo + t - (k - 1):halo + t, :]


def _mixa_sample_kernel(xn_ref, wh_ref, wb_ref, wc_ref, cw_ref, p1_ref, p2_ref, ya_ref, u_ref, *, seq):
    xn = xn_ref[...]
    u = _dot(xn, wc_ref[...]) * _dot(xn, wh_ref[...])
    tpos = lax.broadcasted_iota(jnp.int32, u.shape, 0) % seq
    y = _causal_conv(u, cw_ref[...], tpos, [p1_ref[...], p2_ref[...]])
    ya_ref[...] = (_dot(xn, wb_ref[...]) * y).astype(ya_ref.dtype)
    u_ref[...] = u


def _prev_rows(state, seq):
    b, km1, c = state.shape
    outs = []
    for j in range(1, km1 + 1):
        rows = [state[:, km1 - j + t, :] if t < j else jnp.zeros((b, c), state.dtype) for t in range(seq)]
        outs.append(jnp.stack(rows, axis=1).reshape(b * seq, c))
    return outs


def _new_state(state, u, seq):
    b, km1, c = state.shape
    ext = jnp.concatenate([state, u.reshape(b, seq, c)], axis=1)
    return ext[:, -km1:, :]


def _mixa_prompt(xn, wh, wb, wc, cw, nb, seq):
    m, d = xn.shape
    c = wh.width
    km1 = cw.shape[0] - 1
    tn = _pick(c, (512, 256, 128))
    (wh, oh), (wb, ob), (wc, oc) = (_col_operand(w, tn) for w in (wh, wb, wc))
    wspec = lambda off: pl.BlockSpec((d, tn), lambda b, j: (0, j + off))
    assert km1 <= SUBLANES
    return pl.pallas_call(
        functools.partial(_mixa_prompt_kernel, chunk=_pick(seq, (64, 32, 16, 8))),
        grid=(nb, c // tn),
        in_specs=[pl.BlockSpec((seq, d), lambda b, j: (b, 0)), wspec(oh), wspec(ob), wspec(oc),
                  pl.BlockSpec((km1 + 1, tn), lambda b, j: (0, j))],
        out_specs=[pl.BlockSpec((seq, tn), lambda b, j: (b, j)),
                   pl.BlockSpec((None, km1, tn), lambda b, j: (b, 0, j))],
        out_shape=[jax.ShapeDtypeStruct((m, c), BF16),
                   jax.ShapeDtypeStruct((nb, km1, c), F32)],
        scratch_shapes=[pltpu.VMEM((seq + SUBLANES, tn), F32), pltpu.VMEM((seq, tn), F32)],
        compiler_params=_params(2),
        name="mixa_prompt",
    )(xn, wh, wb, wc, cw)


def _mixa_sample(xn, wh, wb, wc, cw, state, seq):
    m, d = xn.shape
    c = wh.width
    tn = _pick(c, (512, 256, 128))
    p1, p2 = _prev_rows(state, seq)
    (wh, oh), (wb, ob), (wc, oc) = (_col_operand(w, tn) for w in (wh, wb, wc))
    wspec = lambda off: pl.BlockSpec((d, tn), lambda j: (0, j + off))
    cspec = pl.BlockSpec((m, tn), lambda j: (0, j))
    ya, u = pl.pallas_call(
        functools.partial(_mixa_sample_kernel, seq=seq),
        grid=(c // tn,),
        in_specs=[pl.BlockSpec((m, d), lambda j: (0, 0)), wspec(oh), wspec(ob), wspec(oc),
                  pl.BlockSpec((cw.shape[0], tn), lambda j: (0, j)), cspec, cspec],
        out_specs=[cspec, cspec],
        out_shape=[jax.ShapeDtypeStruct((m, c), BF16), jax.ShapeDtypeStruct((m, c), F32)],
        compiler_params=_params(1),
        name="mixa_sample",
    )(xn, wh, wb, wc, cw, p1, p2)
    return ya, _new_state(state, u, seq)


def _qkv_kernel(xn_ref, wq_ref, wk_ref, wv_ref, q_ref, k_ref, v_ref, *, q_scale):
    xn = xn_ref[...]
    q_ref[...] = (_dot(xn, wq_ref[...]) * q_scale).astype(q_ref.dtype)
    k_ref[...] = _dot(xn, wk_ref[...])
    v_ref[...] = _dot(xn, wv_ref[...])


def _qkv(xn, wq, wk, wv, q_scale):
    m, d = xn.shape
    n = wq.width
    tm = _pick(m, (512, 256, 128))
    (wq, oq), (wk, ok), (wv, ov) = (_col_operand(w, n) for w in (wq, wk, wv))
    wspec = lambda off: pl.BlockSpec((d, n), lambda i: (0, off), pipeline_mode=pl.Buffered(1))
    ospec = pl.BlockSpec((tm, n), lambda i: (i, 0))
    return pl.pallas_call(
        functools.partial(_qkv_kernel, q_scale=q_scale),
        grid=(m // tm,),
        in_specs=[pl.BlockSpec((tm, d), lambda i: (i, 0)), wspec(oq), wspec(ok), wspec(ov)],
        out_specs=[ospec, ospec, ospec],
        out_shape=[jax.ShapeDtypeStruct((m, n), BF16), jax.ShapeDtypeStruct((m, n), F32),
                   jax.ShapeDtypeStruct((m, n), F32)],
        compiler_params=_params(1),
        name="qkv_proj",
    )(xn, wq, wk, wv)


def _gates_kernel(x_ref, g_ref, wa_ref, wb_ref, wf_ref, bf_ref, xn_ref, sa_ref, sb_ref, lf_ref):
    xn = _rms_scale(x_ref[...], g_ref[...]).astype(xn_ref.dtype)
    xn_ref[...] = xn
    lf_ref[...] = _log_sigmoid(_dot(xn, wf_ref[...]) + bf_ref[...])
    sa_ref[...] = _sigmoid(_dot(xn, wa_ref[...])).astype(sa_ref.dtype)
    sb_ref[...] = _sigmoid(_dot(xn, wb_ref[...])).astype(sb_ref.dtype)


def _norm_gates(x, g, wa, wb, wf_pad, bf_pad):
    m, d = x.shape
    n = wa.width
    tm = _pick(m, (512, 256, 128))
    wa, wb = (w.array[:, w.start:w.start + w.width] for w in (wa, wb))
    assert 2 * (2 * d * n) <= VMEM_LIMIT // 2
    once = dict(pipeline_mode=pl.Buffered(1))
    rspec = pl.BlockSpec((tm, d), lambda i: (i, 0))
    ospec = pl.BlockSpec((tm, n), lambda i: (i, 0))
    return pl.pallas_call(
        _gates_kernel,
        grid=(m // tm,),
        in_specs=[rspec, pl.BlockSpec((1, d), lambda i: (0, 0)),
                  pl.BlockSpec((d, n), lambda i: (0, 0), **once),
                  pl.BlockSpec((d, n), lambda i: (0, 0), **once),
                  pl.BlockSpec((d, LANES), lambda i: (0, 0)),
                  pl.BlockSpec((1, LANES), lambda i: (0, 0))],
        out_specs=[rspec, ospec, ospec, pl.BlockSpec((tm, LANES), lambda i: (i, 0))],
        out_shape=[jax.ShapeDtypeStruct((m, d), BF16), jax.ShapeDtypeStruct((m, n), BF16),
                   jax.ShapeDtypeStruct((m, n), BF16), jax.ShapeDtypeStruct((m, LANES), F32)],
        compiler_params=_params(1),
        name="norm_gate_proj",
    )(x, g.reshape(1, d), wa, wb, wf_pad, bf_pad)


def _cumsum_kernel(lf_ref, f_ref, *, blk):
    t = lf_ref.shape[0]
    r = lax.broadcasted_iota(jnp.int32, (blk, blk), 0)
    c = lax.broadcasted_iota(jnp.int32, (blk, blk), 1)
    lower = (c <= r).astype(BF16)
    carry = jnp.zeros((1, lf_ref.shape[1]), F32)
    for i in range(t // blk):
        y = _dot01(lower, lf_ref[i * blk:(i + 1) * blk, :], True) + carry
        f_ref[i * blk:(i + 1) * blk, :] = y
        carry = y[blk - 1:blk, :]


def _cumsum_rows(lf, nb, seq):
    blk = _pick(seq, (256, 128))
    return pl.pallas_call(
        functools.partial(_cumsum_kernel, blk=blk),
        grid=(nb,),
        in_specs=[pl.BlockSpec((seq, LANES), lambda b: (b, 0))],
        out_specs=pl.BlockSpec((seq, LANES), lambda b: (b, 0)),
        out_shape=jax.ShapeDtypeStruct(lf.shape, F32),
        compiler_params=_params(1),
        name="logforget_cumsum",
    )(lf)


def _fox_prompt_kernel(q_ref, k_ref, v_ref, fcol_ref, frow_ref, o_ref, *, hg, hd, nq):
    hgrp = pl.program_id(1)
    i = pl.program_id(2)
    tq = q_ref.shape[0]
    lane = lax.broadcasted_iota(jnp.int32, fcol_ref.shape, 1)
    cols_of = lambda e: slice(e * hd, (e + 1) * hd)

    def attend(c):
        past = c * tq
        rows = lax.broadcasted_iota(jnp.int32, (tq, tq), 0)
        cols = lax.broadcasted_iota(jnp.int32, (tq, tq), 1)
        for e in range(hg):
            q = q_ref[:, cols_of(e)]
            fq = jnp.sum(jnp.where(lane == hgrp * hg + e, fcol_ref[...], 0.0),
                         axis=-1, keepdims=True) * LOG2E
            kd = k_ref[past:past + tq, cols_of(e)].astype(BF16)
            td = _dot_nt(q, kd) - frow_ref[e:e + 1, past:past + tq] * LOG2E
            td = jnp.where(cols <= rows, td, -jnp.inf)
            mx = jnp.max(td, axis=-1, keepdims=True)
            if c:
                kp = k_ref[0:past, cols_of(e)].astype(BF16)
                tp = _dot_nt(q, kp) - frow_ref[e:e + 1, 0:past] * LOG2E
                mx = jnp.maximum(mx, jnp.max(tp, axis=-1, keepdims=True))
            shift = (mx + fq) - fq
            pd = jnp.exp2(td - shift)
            l = jnp.sum(pd, axis=-1, keepdims=True)
            acc = _dot(pd.astype(BF16), v_ref[past:past + tq, cols_of(e)].astype(BF16))
            if c:
                pp = jnp.exp2(tp - shift)
                l = l + jnp.sum(pp, axis=-1, keepdims=True)
                acc = acc + _dot(pp.astype(BF16), v_ref[0:past, cols_of(e)].astype(BF16))
            o_ref[:, cols_of(e)] = (acc / l).astype(o_ref.dtype)

    for c in range(nq):
        pl.when(i == c)(functools.partial(attend, c))


def _fox_prompt(q, k, v, fcum, nb, seq, nh, hd):
    m = q.shape[0]
    tq = _pick(seq, (512, 256, 128))
    nq = seq // tq
    hg = _pick(nh, (4, 2, 1))
    ng = nh // hg
    frow = fcum[:, :nh].reshape(nb, seq, nh).transpose(0, 2, 1).reshape(nb * ng, hg, seq)
    return pl.pallas_call(
        functools.partial(_fox_prompt_kernel, hg=hg, hd=hd, nq=nq),
        grid=(nb, ng, nq),
        in_specs=[pl.BlockSpec((tq, hg * hd), lambda b, h, i: (b * nq + i, h)),
                  pl.BlockSpec((seq, hg * hd), lambda b, h, i: (b, h)),
                  pl.BlockSpec((seq, hg * hd), lambda b, h, i: (b, h)),
                  pl.BlockSpec((tq, LANES), lambda b, h, i: (b * nq + i, 0)),
                  pl.BlockSpec((None, hg, seq), lambda b, h, i: (b * ng + h, 0, 0))],
        out_specs=pl.BlockSpec((tq, hg * hd), lambda b, h, i: (b * nq + i, h)),
        out_shape=jax.ShapeDtypeStruct((m, nh * hd), BF16),
        compiler_params=_params(3),
        name="fox_prompt",
    )(q, k, v, fcum, frow)


def _suffix_kernel(lf_ref, o_ref, *, nh):
    w = lf_ref.shape[1]
    wo = o_ref.shape[1]
    shift = nh.bit_length() - 1
    r = lax.broadcasted_iota(jnp.int32, (w, wo), 0)
    c = lax.broadcasted_iota(jnp.int32, (w, wo), 1)
    same_head = jnp.bitwise_and(r, nh - 1) == jnp.bitwise_and(c, nh - 1)
    later_key = lax.shift_right_logical(r, shift) > lax.shift_right_logical(c, shift)
    sel = jnp.logical_and(same_head, jnp.logical_or(later_key, c >= w)).astype(BF16)
    o_ref[...] = _dot01(sel, lf_ref[...], False)


def _page_suffix(cache_lf_l):
    n_pool, page, nh = cache_lf_l.shape
    assert nh & (nh - 1) == 0 and LANES % nh == 0
    w = page * nh
    pp = _pick(n_pool, (512, 256, 128, 64, 32, 16, 8))
    out = pl.pallas_call(
        functools.partial(_suffix_kernel, nh=nh),
        grid=(n_pool // pp,),
        in_specs=[pl.BlockSpec((pp, w), lambda i: (i, 0))],
        out_specs=pl.BlockSpec((pp, w + LANES), lambda i: (i, 0)),
        out_shape=jax.ShapeDtypeStruct((n_pool, w + LANES), F32),
        compiler_params=_params(1),
        name="page_suffix",
    )(cache_lf_l.reshape(n_pool, w))
    return out


def _fox_sample_kernel(pt_ref, q_ref, kn_ref, vn_ref, lfn_ref, *rest, n_pg, nh, page_slot):
    k_refs = rest[:n_pg]
    v_refs = rest[n_pg:2 * n_pg]
    w_refs = rest[2 * n_pg:3 * n_pg]
    o_ref = rest[3 * n_pg]
    fn_ref, m_ref, l_ref, acc_ref, tail_ref, kcat_ref, vcat_ref = rest[3 * n_pg + 1:]
    g = pl.program_id(1)
    rows, hd = q_ref.shape
    t_new = rows // nh
    pw = k_refs[0].shape[0] * nh

    @pl.when(g == 0)
    def _init():
        eye = (lax.broadcasted_iota(jnp.int32, (nh, LANES), 0)
               == lax.broadcasted_iota(jnp.int32, (nh, LANES), 1))
        qf = q_ref[...].astype(F32)
        kn = kn_ref[...]
        vn = vn_ref[...]
        lfn = lfn_ref[...]
        fcol = []
        run = jnp.zeros((1, LANES), F32)
        for t in range(t_new):
            run = run + lfn[t:t + 1, :]
            fcol.append(jnp.sum(jnp.where(eye, jnp.broadcast_to(run, (nh, LANES)), 0.0),
                                axis=-1, keepdims=True) * LOG2E)
        for t in range(t_new):
            rs = slice(t * nh, (t + 1) * nh)
            s = [jnp.sum(qf[rs, :] * kn[j * nh:(j + 1) * nh, :], axis=-1, keepdims=True)
                 + (fcol[t] - fcol[j]) for j in range(t + 1)]
            m = s[0]
            for sj in s[1:]:
                m = jnp.maximum(m, sj)
            l = jnp.zeros((nh, 1), F32)
            acc = jnp.zeros((nh, hd), F32)
            for j, sj in enumerate(s):
                p = jnp.exp2(sj - m)
                l = l + p
                acc = acc + p * vn[j * nh:(j + 1) * nh, :]
            fn_ref[rs, :] = fcol[t]
            m_ref[rs, :] = m
            l_ref[rs, :] = l
            acc_ref[rs, :] = acc
        tail_ref[...] = jnp.zeros_like(tail_ref)

    reps = pw // LANES
    tail = tail_ref[...]
    bias = []
    for j in range(n_pg):
        kcat_ref[j * pw:(j + 1) * pw, :] = k_refs[j][...].reshape(pw, hd).astype(kcat_ref.dtype)
        vcat_ref[j * pw:(j + 1) * pw, :] = v_refs[j][...].reshape(pw, hd).astype(vcat_ref.dtype)
        slot = jnp.bitwise_and(page_slot(pt_ref, pl.program_id(0), g, j), SUBLANES - 1)
        w = w_refs[j][pl.ds(slot, 1), :]
        bias.append(w[:, :pw] + jnp.concatenate([tail] * reps, axis=1))
        tail = tail + w[:, pw:]
    tail_ref[...] = tail
    lane = lax.broadcasted_iota(jnp.int32, (rows, LANES), 1)
    row = lax.broadcasted_iota(jnp.int32, (rows, LANES), 0)
    own = jnp.bitwise_and(lane, nh - 1) == jnp.bitwise_and(row, nh - 1)
    mask = jnp.where(own, 0.0, -jnp.inf)
    bias = jnp.concatenate(bias, axis=1) * LOG2E
    t = _dot_nt(q_ref[...], kcat_ref[...])
    t = t + (bias + jnp.concatenate([mask] * (n_pg * reps), axis=1))
    fn = fn_ref[...]
    m_old = m_ref[...]
    m_new = jnp.maximum(m_old, jnp.max(t, axis=-1, keepdims=True) + fn)
    alpha = jnp.exp2(m_old - m_new)
    p = jnp.exp2(t - (m_new - fn))
    l_ref[...] = alpha * l_ref[...] + jnp.sum(p, axis=-1, keepdims=True)
    acc_ref[...] = alpha * acc_ref[...] + _dot(p.astype(BF16), vcat_ref[...])
    m_ref[...] = m_new

    @pl.when(g == pl.num_programs(1) - 1)
    def _finish():
        o_ref[...] = (acc_ref[...] / l_ref[...]).astype(o_ref.dtype)


def _fox_sample(q, k, v, lf, cache_k, cache_v, layer, wt, page_table, nbd, t_new, nh, hd):
    page = cache_k.shape[2]
    assert nh == SUBLANES, "a cached (head, head_dim) slab must be exactly one f32 vreg tile"
    n_pages = page_table.shape[1]
    n_pg = _pick(n_pages, (16, 8, 4, 2, 1))
    rows = t_new * nh
    pw = page * nh

    assert wt.shape[0] % SUBLANES == 0

    def page_index(pt, b, g, j):
        return pt[b, n_pages - 1 - (g * n_pg + j)]

    def cache_spec(j):
        return pl.BlockSpec((None, None, page, nh, hd),
                            lambda b, g, pt: (layer, page_index(pt, b, g, j), 0, 0, 0))

    def suffix_spec(j):
        return pl.BlockSpec((SUBLANES, pw + LANES),
                            lambda b, g, pt: (lax.shift_right_logical(
                                page_index(pt, b, g, j), SUBLANES.bit_length() - 1), 0))

    new_spec = pl.BlockSpec((None, rows, hd), lambda b, g, pt: (b, 0, 0))
    in_specs = [new_spec, new_spec, new_spec,
                pl.BlockSpec((None, t_new, LANES), lambda b, g, pt: (b, 0, 0))]
    in_specs += [cache_spec(j) for j in range(n_pg)]
    in_specs += [cache_spec(j) for j in range(n_pg)]
    in_specs += [suffix_spec(j) for j in range(n_pg)]
    grid_spec = pltpu.PrefetchScalarGridSpec(
        num_scalar_prefetch=1,
        grid=(nbd, n_pages // n_pg),
        in_specs=in_specs,
        out_specs=new_spec,
        scratch_shapes=[pltpu.VMEM((rows, 1), F32),
                        pltpu.VMEM((rows, 1), F32),
                        pltpu.VMEM((rows, 1), F32),
                        pltpu.VMEM((rows, hd), F32),
                        pltpu.VMEM((1, LANES), F32),
                        pltpu.VMEM((n_pg * pw, hd), BF16),
                        pltpu.VMEM((n_pg * pw, hd), BF16)])
    to_rows = lambda a: a.reshape(nbd, rows, hd)
    out = pl.pallas_call(
        functools.partial(_fox_sample_kernel, n_pg=n_pg, nh=nh, page_slot=page_index),
        grid_spec=grid_spec,
        out_shape=jax.ShapeDtypeStruct((nbd, rows, hd), BF16),
        compiler_params=_params(2),
        name="fox_sample",
    )(page_table, to_rows(q), to_rows(k), to_rows(v), lf.reshape(nbd, t_new, LANES),
      *([cache_k] * n_pg), *([cache_v] * n_pg), *([wt] * n_pg))
    return out.reshape(nbd * t_new, nh * hd)


def _merge_kernel(ya_ref, o_ref, woa_ref, wob_ref, sa_ref, sb_ref, m_ref):
    a = _dot(ya_ref[...], woa_ref[...])
    b = _dot(o_ref[...], wob_ref[...])
    m_ref[...] = (sa_ref[...].astype(F32) * a + sb_ref[...].astype(F32) * b).astype(m_ref.dtype)


def _merge(ya, o, woa, wob, sa, sb):
    m, ca = ya.shape
    cb = o.shape[1]
    n = woa.shape[1]
    tm = _pick(m, (512, 256, 128))
    once = dict(pipeline_mode=pl.Buffered(1))
    gspec = pl.BlockSpec((tm, n), lambda i: (i, 0))
    return pl.pallas_call(
        _merge_kernel,
        grid=(m // tm,),
        in_specs=[pl.BlockSpec((tm, ca), lambda i: (i, 0)),
                  pl.BlockSpec((tm, cb), lambda i: (i, 0)),
                  pl.BlockSpec((ca, n), lambda i: (0, 0), **once),
                  pl.BlockSpec((cb, n), lambda i: (0, 0), **once), gspec, gspec],
        out_specs=gspec,
        out_shape=jax.ShapeDtypeStruct((m, n), BF16),
        compiler_params=_params(1),
        name="mixer_merge",
    )(ya, o, woa, wob, sa, sb)


def _oproj_kernel(mix_ref, wo_ref, x_ref, g_ref, x1_ref, xn_ref):
    x1 = x_ref[...] + _dot(mix_ref[...], wo_ref[...])
    x1_ref[...] = x1
    xn_ref[...] = _rms_scale(x1, g_ref[...]).astype(xn_ref.dtype)


def _oproj(mix, wo, x, g):
    m, d = x.shape
    tm = _pick(m, (512, 256, 128))
    rspec = pl.BlockSpec((tm, d), lambda i: (i, 0))
    return pl.pallas_call(
        _oproj_kernel,
        grid=(m // tm,),
        in_specs=[rspec, pl.BlockSpec((d, d), lambda i: (0, 0)), rspec,
                  pl.BlockSpec((1, d), lambda i: (0, 0))],
        out_specs=[rspec, rspec],
        out_shape=[jax.ShapeDtypeStruct((m, d), F32), jax.ShapeDtypeStruct((m, d), BF16)],
        compiler_params=_params(1),
        name="oproj_residual_norm",
    )(mix, wo, x, g.reshape(1, d))


def _ffn_up_prompt_kernel(xn_ref, wu_ref, wg_ref, cw_ref, h_ref, tail_ref, g_scr, u_scr, *, chunk):
    xn = xn_ref[...]
    t, tn = u_scr.shape
    halo = g_scr.shape[0] - t
    k = cw_ref.shape[0]
    g_scr[0:halo, :] = jnp.zeros((halo, tn), F32)
    g_scr[halo:, :] = _dot(xn, wg_ref[...])
    u_scr[...] = _dot(xn, wu_ref[...])
    cw = cw_ref[...]
    taps = [jnp.broadcast_to(cw[i:i + 1, :], (chunk, tn)) for i in range(k)]
    for r0 in range(0, t, chunk):
        gc = _conv_taps(g_scr, halo, r0, chunk, taps)
        h_ref[r0:r0 + chunk, :] = (gc * _sigmoid(gc) * u_scr[r0:r0 + chunk, :]).astype(h_ref.dtype)
    tail_ref[...] = g_scr[halo + t - (k - 1):halo + t, :]


def _ffn_up_sample_kernel(xn_ref, wu_ref, wg_ref, cw_ref, p1_ref, p2_ref, h_ref, g_ref, *, seq):
    xn = xn_ref[...]
    gpre = _dot(xn, wg_ref[...])
    tpos = lax.broadcasted_iota(jnp.int32, gpre.shape, 0) % seq
    gc = _causal_conv(gpre, cw_ref[...], tpos, [p1_ref[...], p2_ref[...]])
    h_ref[...] = (gc * _sigmoid(gc) * _dot(xn, wu_ref[...])).astype(h_ref.dtype)
    g_ref[...] = gpre


def _ffn_up_prompt(xn, wu, wg, cw, nb, seq):
    m, d = xn.shape
    f = wu.shape[1]
    km1 = cw.shape[0] - 1
    tn = _pick(f, (512, 256, 128))
    wspec = pl.BlockSpec((d, tn), lambda b, j: (0, j))
    assert km1 <= SUBLANES
    return pl.pallas_call(
        functools.partial(_ffn_up_prompt_kernel, chunk=_pick(seq, (64, 32, 16, 8))),
        grid=(nb, f // tn),
        in_specs=[pl.BlockSpec((seq, d), lambda b, j: (b, 0)), wspec, wspec,
                  pl.BlockSpec((km1 + 1, tn), lambda b, j: (0, j))],
        out_specs=[pl.BlockSpec((seq, tn), lambda b, j: (b, j)),
                   pl.BlockSpec((None, km1, tn), lambda b, j: (b, 0, j))],
        out_shape=[jax.ShapeDtypeStruct((m, f), BF16), jax.ShapeDtypeStruct((nb, km1, f), F32)],
        scratch_shapes=[pltpu.VMEM((seq + SUBLANES, tn), F32), pltpu.VMEM((seq, tn), F32)],
        compiler_params=_params(2),
        name="ffn_up_prompt",
    )(xn, wu, wg, cw)


def _ffn_up_sample(xn, wu, wg, cw, state, seq):
    m, d = xn.shape
    f = wu.shape[1]
    tn = _pick(f, (512, 256, 128))
    p1, p2 = _prev_rows(state, seq)
    wspec = pl.BlockSpec((d, tn), lambda j: (0, j))
    cspec = pl.BlockSpec((m, tn), lambda j: (0, j))
    h, gpre = pl.pallas_call(
        functools.partial(_ffn_up_sample_kernel, seq=seq),
        grid=(f // tn,),
        in_specs=[pl.BlockSpec((m, d), lambda j: (0, 0)), wspec, wspec,
                  pl.BlockSpec((cw.shape[0], tn), lambda j: (0, j)), cspec, cspec],
        out_specs=[cspec, cspec],
        out_shape=[jax.ShapeDtypeStruct((m, f), BF16), jax.ShapeDtypeStruct((m, f), F32)],
        compiler_params=_params(1),
        name="ffn_up_sample",
    )(xn, wu, wg, cw, p1, p2)
    return h, _new_state(state, gpre, seq)


def _ffn_down_kernel(h_ref, wd_ref, x1_ref, g_ref, y_ref, *, normalize):
    x2 = x1_ref[...] + _dot(h_ref[...], wd_ref[...])
    y_ref[...] = _rms_scale(x2, g_ref[...]) if normalize else x2


def _ffn_down(h, wd, x1, g, normalize):
    m, f = h.shape
    d = x1.shape[1]
    assert 2 * f * d <= VMEM_LIMIT // 2, "w_down must fit in VMEM next to the row tiles"
    tm = _pick(m, (512, 256, 128))
    rspec = pl.BlockSpec((tm, d), lambda i: (i, 0))
    return pl.pallas_call(
        functools.partial(_ffn_down_kernel, normalize=normalize),
        grid=(m // tm,),
        in_specs=[pl.BlockSpec((tm, f), lambda i: (i, 0)),
                  pl.BlockSpec((f, d), lambda i: (0, 0), pipeline_mode=pl.Buffered(1)),
                  rspec, pl.BlockSpec((1, d), lambda i: (0, 0))],
        out_specs=rspec,
        out_shape=jax.ShapeDtypeStruct((m, d), F32),
        compiler_params=_params(1),
        name="ffn_down_residual_norm",
    )(h, wd, x1, g.reshape(1, d))


def _in_proj_weights(w_in_l, b_f_l, d_conv, d_attn, nh, d_model):
    w = w_in_l.astype(BF16)
    sizes = (d_conv, d_conv, d_conv, d_attn, d_attn, d_attn, nh, d_model, d_model)
    cols, start = [], 0
    for s in sizes:
        cols.append(_Cols(w, start, s))
        start += s
    wf = cols[6]
    wf_pad = jnp.pad(w[:, wf.start:wf.start + nh], ((0, 0), (0, LANES - nh)))
    bf_pad = jnp.pad(b_f_l.astype(F32), (0, LANES - nh)).reshape(1, LANES)
    return cols[:6], cols[7:], wf_pad, bf_pad


def kernel(x_prompt, x_sample, cache_k, cache_v, cache_lf, state_conv_a, state_conv_ffn, page_table,
           norm_mix_g, w_in, b_f, conv_a_w, w_out_a, w_out_b, w_o, norm_ffn_g, w_up, w_gate,
           conv_ffn_w, w_down, norm_final_g):
    bp, tp, d_model = x_prompt.shape
    bd, td, _ = x_sample.shape
    depth = w_in.shape[0]
    nh, hd = cache_k.shape[3], cache_k.shape[4]
    d_attn = nh * hd
    d_conv = conv_a_w.shape[2]
    assert conv_a_w.shape[1] == 3 and conv_ffn_w.shape[1] == 3
    assert nh <= SUBLANES and hd % LANES == 0
    q_scale = hd ** -0.5 * LOG2E

    hp = x_prompt.reshape(bp * tp, d_model)
    hs = x_sample.reshape(bd * td, d_model)
    outs = [[] for _ in range(10)]
    for l in range(depth):
        (wh, wb, wc, wq, wk, wv), (wga, wgb), wf_pad, bf_pad = _in_proj_weights(
            w_in[l], b_f[l], d_conv, d_attn, nh, d_model)
        woa, wob, wo = w_out_a[l].astype(BF16), w_out_b[l].astype(BF16), w_o[l].astype(BF16)
        wu, wg, wd = w_up[l].astype(BF16), w_gate[l].astype(BF16), w_down[l].astype(BF16)
        last = l == depth - 1

        xn, sa, sb, lfp = _norm_gates(hp, norm_mix_g[l], wga, wgb, wf_pad, bf_pad)
        ya, cap = _mixa_prompt(xn, wh, wb, wc, conv_a_w[l], bp, tp)
        q, kp, vp = _qkv(xn, wq, wk, wv, q_scale)
        fcum = _cumsum_rows(lfp, bp, tp)
        o = _fox_prompt(q, kp, vp, fcum, bp, tp, nh, hd)
        mix = _merge(ya, o, woa, wob, sa, sb)
        x1, xn2 = _oproj(mix, wo, hp, norm_ffn_g[l])
        hact, cfp = _ffn_up_prompt(xn2, wu, wg, conv_ffn_w[l], bp, tp)
        hp = _ffn_down(hact, wd, x1, norm_final_g, last)

        xn, sa, sb, lfs = _norm_gates(hs, norm_mix_g[l], wga, wgb, wf_pad, bf_pad)
        ya, cas = _mixa_sample(xn, wh, wb, wc, conv_a_w[l], state_conv_a[l], td)
        q, ks, vs = _qkv(xn, wq, wk, wv, q_scale)
        wt = _page_suffix(cache_lf[l])
        o = _fox_sample(q, ks, vs, lfs, cache_k, cache_v, l, wt, page_table, bd, td, nh, hd)
        mix = _merge(ya, o, woa, wob, sa, sb)
        x1s, xn2 = _oproj(mix, wo, hs, norm_ffn_g[l])
        hact, cfs = _ffn_up_sample(xn2, wu, wg, conv_ffn_w[l], state_conv_ffn[l], td)
        hs = _ffn_down(hact, wd, x1s, norm_final_g, last)

        for lst, val in zip(outs, (kp.reshape(bp, tp, nh, hd), vp.reshape(bp, tp, nh, hd),
                                   lfp[:, :nh].reshape(bp, tp, nh), ks.reshape(bd, td, nh, hd),
                                   vs.reshape(bd, td, nh, hd), lfs[:, :nh].reshape(bd, td, nh),
                                   cap, cas, cfp, cfs)):
            lst.append(val)
    return (hp.reshape(bp, tp, d_model), hs.reshape(bd, td, d_model), *[jnp.stack(o_) for o_ in outs])
```
